```python
import jax
import jax.numpy as jnp
from jax import lax
import numpy as np

D_MODEL = 2048
BATCH = 4
SEQ = 4096
DEPTH = 4

N_MEM = 256
CHUNK = 64
EPS = 1e-6
A_HEADS = 8
A_DK = 128
A_DV = 128
A_CONV = 4
B_HEADS = 8
B_DK = 128
B_DV = 128
C_HEADS = 4
C_DH = 256
N_BRANCH = 3
D_FF = 4 * D_MODEL
A_QK_W = A_HEADS * A_DK
A_V_W = A_HEADS * A_DV
B_K_W = B_HEADS * B_DK
B_V_W = B_HEADS * B_DV
C_W = C_HEADS * C_DH
IN_SIZES = (A_QK_W, A_QK_W, A_V_W, A_V_W, A_HEADS, A_HEADS, B_K_W, B_K_W, B_V_W, B_V_W, C_W, N_BRANCH * D_MODEL)
IN_TOTAL = sum(IN_SIZES)

kernel_name = 'hybrid_deltanet_hgrn2_memxattn_block'


def rmsnorm(x, g):
    xf = x.astype(jnp.float32)
    y = xf * lax.rsqrt(jnp.mean(xf * xf, axis=-1, keepdims=True) + EPS)
    return (y * g.astype(jnp.float32)).astype(x.dtype)


def gated_head_norm(o, z, g):
    of = o.astype(jnp.float32)
    y = of * lax.rsqrt(jnp.mean(of * of, axis=-1, keepdims=True) + EPS)
    return y * g.astype(jnp.float32) * jax.nn.silu(z.astype(jnp.float32))


def l2norm(u):
    return u * lax.rsqrt(jnp.sum(u * u, axis=-1, keepdims=True) + EPS)


def causal_dwconv(u, w):
    k = w.shape[0]
    return lax.conv_general_dilated(u, w[:, None, :], window_strides=(1,), padding=[(k - 1, 0)], dimension_numbers=('NWC', 'WIO', 'NWC'), feature_group_count=u.shape[-1])


def to_chunks(u):
    b, s, h, d = u.shape
    return u.reshape(b, s // CHUNK, CHUNK, h, d).transpose(0, 3, 1, 2, 4)


def from_chunks(o):
    n, b, h, c, d = o.shape
    return o.transpose(1, 0, 3, 2, 4).reshape(b, n * c, h, d)


def gated_deltanet(q, k, v, beta, logdecay):
    dk = q.shape[-1]
    dv = v.shape[-1]
    qc = to_chunks(l2norm(q) * (dk ** -0.5))
    kc = to_chunks(l2norm(k))
    vc = to_chunks(v)
    bc = to_chunks(beta[..., None])[..., 0]
    gam = jnp.cumsum(to_chunks(logdecay[..., None])[..., 0], axis=-1)
    causal = jnp.tril(jnp.ones((CHUNK, CHUNK), dtype=bool))
    strict = jnp.tril(jnp.ones((CHUNK, CHUNK), dtype=bool), k=-1)
    dec = jnp.exp(jnp.where(causal, gam[..., :, None] - gam[..., None, :], -jnp.inf))
    lower = jnp.where(strict, bc[..., :, None] * jnp.einsum('bhnid,bhnjd->bhnij', kc, kc) * dec, 0.0)
    a_mat = lower + jnp.eye(CHUNK, dtype=lower.dtype)
    rhs = jnp.concatenate([vc * bc[..., None], kc * (bc * jnp.exp(gam))[..., None]], axis=-1)
    sol = lax.linalg.triangular_solve(a_mat, rhs, left_side=True, lower=True, unit_diagonal=True)
    u_c = sol[..., :dv]
    w_c = sol[..., dv:]
    a_qk = jnp.einsum('bhnid,bhnjd->bhnij', qc, kc) * dec
    q_dec = qc * jnp.exp(gam)[..., None]
    k_dec = kc * jnp.exp(gam[..., -1:] - gam)[..., None]
    g_last = jnp.exp(gam[..., -1])

    def step(state, inp):
        q_i, k_i, u_i, w_i, a_i, g_i = inp
        u_new = u_i - w_i @ state
        o = q_i @ state + a_i @ u_new
        state = g_i[..., None, None] * state + jnp.swapaxes(k_i, -1, -2) @ u_new
        return state, o

    xs = tuple(jnp.moveaxis(t, 2, 0) for t in (q_dec, k_dec, u_c, w_c, a_qk, g_last))
    s0 = jnp.zeros((q_dec.shape[0], q_dec.shape[1], dk, dv), jnp.float32)
    _, o = lax.scan(step, s0, xs)
    return from_chunks(o)


def hgrn2(q, f, i, lb):
    h, dk = q.shape[-2], q.shape[-1]
    lb = lb.reshape(h, dk)
    log_f = jnp.logaddexp(jnp.log(lb), jnp.log1p(-lb) + jax.nn.log_sigmoid(f))
    k = -jnp.expm1(log_f)
    qc = to_chunks(jax.nn.silu(q) * (dk ** -0.5))
    kc = to_chunks(k)
    vc = to_chunks(i)
    bcum = jnp.cumsum(to_chunks(log_f), axis=-2)
    causal = jnp.tril(jnp.ones((CHUNK, CHUNK), dtype=bool))

    def step(state, inp):
        q_i, k_i, v_i, b_i = inp
        dec = jnp.exp(jnp.where(causal[:, :, None], b_i[..., :, None, :] - b_i[..., None, :, :], -jnp.inf))
        a_i = jnp.einsum('bhtd,bhjd,bhtjd->bhtj', q_i, k_i, dec)
        o = (q_i * jnp.exp(b_i)) @ state + a_i @ v_i
        b_last = b_i[..., -1:, :]
        state = jnp.exp(b_last[..., 0, :])[..., None] * state + jnp.einsum('bhjd,bhje->bhde', k_i * jnp.exp(b_last - b_i), v_i)
        return state, o

    xs = tuple(jnp.moveaxis(t, 2, 0) for t in (qc, kc, vc, bcum))
    s0 = jnp.zeros((qc.shape[0], qc.shape[1], dk, i.shape[-1]), jnp.float32)
    _, o = lax.scan(step, s0, xs)
    return from_chunks(o)


def memory_attention(q, mem_n, w_kv):
    kv = mem_n @ w_kv
    k, v = jnp.split(kv, 2, axis=-1)
    b, m = k.shape[0], k.shape[1]
    k = k.reshape(b, m, C_HEADS, C_DH)
    v = v.reshape(b, m, C_HEADS, C_DH)
    s = jnp.einsum('bshd,bmhd->bhsm', q, k).astype(jnp.float32) * (C_DH ** -0.5)
    p = jax.nn.softmax(s, axis=-1)
    return jnp.einsum('bhsm,bmhd->bshd', p.astype(v.dtype), v)


def setup_inputs(seed: int = 0) -> dict:
    key = jax.random.key(seed)
    ks = jax.random.split(key, 24)
    f32 = jnp.float32

    def nrm(k, shape, scale):
        return jax.random.normal(k, shape, f32) * scale

    def gain(k, shape):
        return 1.0 + 0.02 * jax.random.normal(k, shape, f32)

    x = nrm(ks[0], (BATCH, SEQ, D_MODEL), 1.0)
    mem = nrm(ks[1], (BATCH, N_MEM, D_MODEL), 1.0)
    g_pre_mix = gain(ks[2], (DEPTH, D_MODEL))
    w_in = nrm(ks[3], (DEPTH, D_MODEL, IN_TOTAL), D_MODEL ** -0.5)
    conv_a = nrm(ks[4], (DEPTH, A_CONV, 2 * A_QK_W + A_V_W), A_CONV ** -0.5)
    a_log = jnp.log(jax.random.uniform(ks[5], (DEPTH, A_HEADS), f32, 1.0, 16.0))
    dt = jnp.exp(jax.random.uniform(ks[6], (DEPTH, A_HEADS), f32, float(np.log(1e-3)), float(np.log(1e-1))))
    dt_bias = dt + jnp.log(-jnp.expm1(-dt))
    gn_a = gain(ks[7], (DEPTH, A_DV))
    lb_raw = nrm(ks[8], (DEPTH, B_K_W), 0.1)
    gn_b = gain(ks[9], (DEPTH, B_DV))
    g_mem = gain(ks[10], (DEPTH, D_MODEL))
    w_kv_mem = nrm(ks[11], (DEPTH, D_MODEL, 2 * C_W), D_MODEL ** -0.5)
    w_br_a = nrm(ks[12], (DEPTH, A_V_W, D_MODEL), A_V_W ** -0.5)
    w_br_b = nrm(ks[13], (DEPTH, B_V_W, D_MODEL), B_V_W ** -0.5)
    w_br_c = nrm(ks[14], (DEPTH, C_W, D_MODEL), C_W ** -0.5)
    w_out = nrm(ks[15], (DEPTH, D_MODEL, D_MODEL), D_MODEL ** -0.5)
    g_post_mix = gain(ks[16], (DEPTH, D_MODEL))
    g_pre_mlp = gain(ks[17], (DEPTH, D_MODEL))
    w_mlp_in = nrm(ks[18], (DEPTH, D_MODEL, D_FF), D_MODEL ** -0.5)
    w_mlp_out = nrm(ks[19], (DEPTH, D_FF, D_MODEL), D_FF ** -0.5)
    g_post_mlp = gain(ks[20], (DEPTH, D_MODEL))
    return {'x': x, 'mem': mem, 'g_pre_mix': g_pre_mix, 'w_in': w_in, 'conv_a': conv_a, 'a_log': a_log, 'dt_bias': dt_bias, 'gn_a': gn_a, 'lb_raw': lb_raw, 'gn_b': gn_b, 'g_mem': g_mem, 'w_kv_mem': w_kv_mem, 'w_br_a': w_br_a, 'w_br_b': w_br_b, 'w_br_c': w_br_c, 'w_out': w_out, 'g_post_mix': g_post_mix, 'g_pre_mlp': g_pre_mlp, 'w_mlp_in': w_mlp_in, 'w_mlp_out': w_mlp_out, 'g_post_mlp': g_post_mlp}


def reference(x, mem, g_pre_mix, w_in, conv_a, a_log, dt_bias, gn_a, lb_raw, gn_b, g_mem, w_kv_mem, w_br_a, w_br_b, w_br_c, w_out, g_post_mix, g_pre_mlp, w_mlp_in, w_mlp_out, g_post_mlp):
    b, s, _ = x.shape
    split_idx = np.cumsum(IN_SIZES)[:-1].tolist()
    lb_all = jnp.cumsum(jax.nn.softmax(lb_raw.astype(jnp.float32), axis=0), axis=0)
    lb_all = lb_all - lb_all[0]
    for l in range(DEPTH):
        h = rmsnorm(x, g_pre_mix[l])
        proj = h @ w_in[l]
        aq, ak, av, az, ab, aa, bq, bf, bi, bz, cq, gates = jnp.split(proj, split_idx, axis=-1)
        qkv = jax.nn.silu(causal_dwconv(jnp.concatenate([aq, ak, av], axis=-1), conv_a[l]))
        aq, ak, av = jnp.split(qkv, [A_QK_W, 2 * A_QK_W], axis=-1)
        beta = jax.nn.sigmoid(ab.astype(jnp.float32))
        logdecay = -jnp.exp(a_log[l].astype(jnp.float32)) * jax.nn.softplus(aa.astype(jnp.float32) + dt_bias[l].astype(jnp.float32))
        oa = gated_deltanet(aq.reshape(b, s, A_HEADS, A_DK).astype(jnp.float32), ak.reshape(b, s, A_HEADS, A_DK).astype(jnp.float32), av.reshape(b, s, A_HEADS, A_DV).astype(jnp.float32), beta, logdecay)
        oa = gated_head_norm(oa, az.reshape(b, s, A_HEADS, A_DV), gn_a[l]).reshape(b, s, A_V_W).astype(x.dtype)
        ob = hgrn2(bq.reshape(b, s, B_HEADS, B_DK).astype(jnp.float32), bf.reshape(b, s, B_HEADS, B_DK).astype(jnp.float32), bi.reshape(b, s, B_HEADS, B_DV).astype(jnp.float32), lb_all[l])
        ob = gated_head_norm(ob, bz.reshape(b, s, B_HEADS, B_DV), gn_b[l]).reshape(b, s, B_V_W).astype(x.dtype)
        mem_n = rmsnorm(mem, g_mem[l])
        oc = memory_attention(cq.reshape(b, s, C_HEADS, C_DH), mem_n, w_kv_mem[l]).reshape(b, s, C_W).astype(x.dtype)
        ga, gb, gc = jnp.split(jax.nn.sigmoid(gates), N_BRANCH, axis=-1)
        merged = ga * (oa @ w_br_a[l]) + gb * (ob @ w_br_b[l]) + gc * (oc @ w_br_c[l])
        x = x + rmsnorm(merged @ w_out[l], g_post_mix[l])
        h2 = rmsnorm(x, g_pre_mlp[l])
        m = jnp.square(jax.nn.relu(h2 @ w_mlp_in[l])) @ w_mlp_out[l]
        x = x + rmsnorm(m, g_post_mlp[l])
    return x
```

```python
import functools

import jax
import jax.numpy as jnp
from jax import lax
from jax.experimental import pallas as pl
from jax.experimental.pallas import tpu as pltpu

F32 = jnp.float32
BF16 = jnp.bfloat16

EPS = 1e-6
CHUNK = 64
SUB = 16
HEAD_DIM = 128
N_HEADS = 8
CONV_TAPS = 4
MEM_HEADS = 4
MEM_HEAD_DIM = 256
N_BRANCH = 3
QKVZ_W = 4 * N_HEADS * HEAD_DIM
SMALL_W = 128
VMEM_LIMIT = 56 * 1024 * 1024


def _params(*sem):
    return pltpu.CompilerParams(dimension_semantics=sem, vmem_limit_bytes=VMEM_LIMIT)


def _sigmoid(x):
    return 1.0 / (1.0 + jnp.exp(-x))


def _silu(x):
    return x * _sigmoid(x)


def _rms(x, g):
    return x * lax.rsqrt(jnp.mean(x * x, axis=-1, keepdims=True) + EPS) * g


def _dot(a, b):
    return jnp.dot(a.astype(BF16), b.astype(BF16), preferred_element_type=F32)


def _dot_nt(a, b):
    return lax.dot_general(a.astype(BF16), b.astype(BF16), (((1,), (1,)), ((), ())),
                           preferred_element_type=F32)


def _dot_tn(a, b):
    return lax.dot_general(a.astype(BF16), b.astype(BF16), (((0,), (0,)), ((), ())),
                           preferred_element_type=F32)


def _split3(x):
    hi = x.astype(BF16)
    r1 = x - hi.astype(F32)
    mid = r1.astype(BF16)
    lo = (r1 - mid.astype(F32)).astype(BF16)
    return hi, mid, lo


def _dot_sel(sel, x):
    hi, mid, lo = _split3(x)
    d = lambda t: jnp.dot(sel, t, preferred_element_type=F32)
    return d(hi) + d(mid) + d(lo)


def _dot_hp(a, b):
    a_hi = a.astype(BF16)
    a_lo = (a - a_hi.astype(F32)).astype(BF16)
    b_hi = b.astype(BF16)
    b_lo = (b - b_hi.astype(F32)).astype(BF16)
    d = lambda s, t: jnp.dot(s, t, preferred_element_type=F32)
    return d(a_hi, b_hi) + (d(a_hi, b_lo) + d(a_lo, b_hi))


def _tril_ones(n):
    r = lax.broadcasted_iota(jnp.int32, (n, n), 0)
    c = lax.broadcasted_iota(jnp.int32, (n, n), 1)
    return (r >= c).astype(BF16)


def _inproj_kernel(x_ref, g_ref, w_ref, ws_ref, o_ref, os_ref, h_scr):
    @pl.when(pl.program_id(1) == 0)
    def _():
        h = _rms(x_ref[...], g_ref[...]).astype(BF16)
        h_scr[...] = h
        os_ref[...] = jnp.dot(h, ws_ref[...], preferred_element_type=F32)

    o_ref[...] = jnp.dot(h_scr[...], w_ref[...], preferred_element_type=F32)


def _inproj(x2, g, w_main, w_small, layer, tm=1024, tn=1024):
    t, d = x2.shape
    n = w_main.shape[-1]
    return pl.pallas_call(
        _inproj_kernel,
        name="inproj",
        grid=(t // tm, n // tn),
        in_specs=[
            pl.BlockSpec((tm, d), lambda i, j: (i, 0)),
            pl.BlockSpec((None, 1, d), lambda i, j: (layer, 0, 0)),
            pl.BlockSpec((None, d, tn), lambda i, j: (layer, 0, j)),
            pl.BlockSpec((None, d, SMALL_W), lambda i, j: (layer, 0, 0)),
        ],
        out_specs=[
            pl.BlockSpec((tm, tn), lambda i, j: (i, j)),
            pl.BlockSpec((tm, SMALL_W), lambda i, j: (i, 0)),
        ],
        out_shape=[
            jax.ShapeDtypeStruct((t, n), F32),
            jax.ShapeDtypeStruct((t, SMALL_W), F32),
        ],
        scratch_shapes=[pltpu.VMEM((tm, d), BF16)],
        compiler_params=_params("parallel", "arbitrary"),
    )(x2, g, w_main, w_small)


def _gated_head_norm(o, z, g):
    return o * lax.rsqrt(jnp.mean(o * o, axis=-1, keepdims=True) + EPS) * g * _silu(z)


def _unit_lower_inverse(low):
    n = low.shape[0]
    r = lax.broadcasted_iota(jnp.int32, (n, n), 0)
    c = lax.broadcasted_iota(jnp.int32, (n, n), 1)
    eye = (r == c).astype(F32)
    size = SUB
    same = lambda s: (r >> (s.bit_length() - 1)) == (c >> (s.bit_length() - 1))
    blk = jnp.where(same(size), low, 0.0)
    p2 = _dot_hp(blk, blk)
    p4 = _dot_hp(p2, p2)
    p8 = _dot_hp(p4, p4)
    inv = eye - blk
    inv = inv + _dot_hp(inv, p2)
    inv = inv + _dot_hp(inv, p4)
    inv = inv + _dot_hp(inv, p8)
    while size < n:
        off = jnp.where(same(2 * size), jnp.where(same(size), 0.0, low), 0.0)
        inv = inv - _dot_hp(_dot_hp(inv, off), inv)
        size *= 2
    return inv


def _deltanet_kernel(qkvz_ref, small_ref, conv_ref, alog_ref, dtb_ref, gn_ref, o_ref,
                     state_scr, ext_scr):
    hw = N_HEADS * HEAD_DIM

    @pl.when(pl.program_id(1) == 0)
    def _():
        state_scr[...] = jnp.zeros_like(state_scr)
        ext_scr[0:8, :] = jnp.zeros((8, 3 * hw), F32)

    ext_scr[8:8 + CHUNK, :] = qkvz_ref[:, 0:3 * hw]

    r = lax.broadcasted_iota(jnp.int32, (CHUNK, CHUNK), 0)
    c = lax.broadcasted_iota(jnp.int32, (CHUNK, CHUNK), 1)
    causal = r >= c
    strict = r > c

    sm = small_ref[...]
    beta_all = _sigmoid(sm)
    pre = sm + dtb_ref[...]
    softplus = jnp.maximum(pre, 0.0) + jnp.log(1.0 + jnp.exp(-jnp.abs(pre)))
    logdecay = -jnp.exp(alog_ref[...]) * softplus
    gam_all = _dot_sel(_tril_ones(CHUNK), logdecay)
    gam_t = jnp.concatenate([gam_all, jnp.zeros_like(gam_all)], axis=0).T

    for h in range(N_HEADS):
        lo, hi = h * HEAD_DIM, (h + 1) * HEAD_DIM

        def conv(sec):
            base = sec * hw
            acc = None
            for k in range(CONV_TAPS):
                row0 = 8 - (CONV_TAPS - 1) + k
                term = conv_ref[k:k + 1, base + lo:base + hi] * ext_scr[row0:row0 + CHUNK, base + lo:base + hi]
                acc = term if acc is None else acc + term
            return _silu(acc)

        q = conv(0)
        k = conv(1)
        v = conv(2)
        q = q * lax.rsqrt(jnp.sum(q * q, axis=-1, keepdims=True) + EPS) * (HEAD_DIM ** -0.5)
        k = k * lax.rsqrt(jnp.sum(k * k, axis=-1, keepdims=True) + EPS)

        beta = beta_all[:, h:h + 1]
        gcol = gam_all[:, N_HEADS + h:N_HEADS + h + 1]
        grow = gam_t[N_HEADS + h:N_HEADS + h + 1, 0:CHUNK]
        eg = jnp.exp(gcol)
        dec = jnp.exp(jnp.where(causal, gcol - grow, -jnp.inf))
        kb = k * beta
        kk = _dot_nt(kb, k)
        low = jnp.where(strict, kk * dec, 0.0)
        inv = _unit_lower_inverse(low)
        rhs = jnp.concatenate([v * beta, kb * eg], axis=-1)
        sol = _dot(inv, rhs)
        u = sol[:, :HEAD_DIM]
        w = sol[:, HEAD_DIM:]
        a_qk = _dot_nt(q, k) * dec

        state = state_scr[h]
        u_new = u - _dot(w, state)
        o = _dot(q * eg, state) + _dot(a_qk, u_new)
        g_last = eg[CHUNK - 1:CHUNK, :]
        k_dec = k * jnp.exp(gcol[CHUNK - 1:CHUNK, :] - gcol)
        state_scr[h] = g_last * state + _dot_tn(k_dec, u_new)

        z = qkvz_ref[:, 3 * hw + lo:3 * hw + hi]
        o_ref[:, lo:hi] = _gated_head_norm(o, z, gn_ref[...]).astype(o_ref.dtype)

    ext_scr[0:8, :] = ext_scr[CHUNK:CHUNK + 8, :]


def _deltanet(proj, small, conv_a, alog_row, dtb_row, gn_a, layer, batch):
    t = proj.shape[0]
    n_chunks = t // batch // CHUNK
    hw = N_HEADS * HEAD_DIM
    return pl.pallas_call(
        _deltanet_kernel,
        name="deltanet",
        grid=(batch, n_chunks),
        in_specs=[
            pl.BlockSpec((CHUNK, QKVZ_W), lambda b, n: (b * n_chunks + n, 0)),
            pl.BlockSpec((CHUNK, SMALL_W), lambda b, n: (b * n_chunks + n, 0)),
            pl.BlockSpec((None, CONV_TAPS, 3 * hw), lambda b, n: (layer, 0, 0)),
            pl.BlockSpec((None, 1, SMALL_W), lambda b, n: (layer, 0, 0)),
            pl.BlockSpec((None, 1, SMALL_W), lambda b, n: (layer, 0, 0)),
            pl.BlockSpec((None, 1, HEAD_DIM), lambda b, n: (layer, 0, 0)),
        ],
        out_specs=pl.BlockSpec((CHUNK, hw), lambda b, n: (b * n_chunks + n, 0)),
        out_shape=jax.ShapeDtypeStruct((t, hw), BF16),
        scratch_shapes=[
            pltpu.VMEM((N_HEADS, HEAD_DIM, HEAD_DIM), F32),
            pltpu.VMEM((CHUNK + 8, 3 * hw), F32),
        ],
        compiler_params=_params("parallel", "arbitrary"),
    )(proj, small, conv_a, alog_row, dtb_row, gn_a)


def _hgrn2_kernel(qfiz_ref, lb_ref, gn_ref, o_ref, state_scr, q_scr, b_scr, acc_scr):
    hw = N_HEADS * HEAD_DIM
    n_sub = CHUNK // SUB

    @pl.when(pl.program_id(1) == 0)
    def _():
        state_scr[...] = jnp.zeros_like(state_scr)

    tril = _tril_ones(CHUNK)
    jrow = lax.broadcasted_iota(jnp.int32, (SUB, 1), 0)

    def head(h, carry):
        off = pl.multiple_of(h * HEAD_DIM, HEAD_DIM)
        lb = lb_ref[:, pl.ds(off, HEAD_DIM)]
        f = qfiz_ref[:, pl.ds(hw + off, HEAD_DIM)]
        v = qfiz_ref[:, pl.ds(2 * hw + off, HEAD_DIM)]
        z = qfiz_ref[:, pl.ds(3 * hw + off, HEAD_DIM)]
        q = _silu(qfiz_ref[:, pl.ds(off, HEAD_DIM)]) * (HEAD_DIM ** -0.5)

        log_sig = jnp.minimum(f, 0.0) - jnp.log(1.0 + jnp.exp(-jnp.abs(f)))
        la = jnp.log(lb)
        lc = jnp.log(1.0 - lb) + log_sig
        log_f = jnp.maximum(la, lc) + jnp.log(1.0 + jnp.exp(-jnp.abs(la - lc)))
        kk = (1.0 - lb) / (1.0 + jnp.exp(f))
        bcum = _dot_sel(tril, log_f)
        b_last = bcum[CHUNK - 1:CHUNK, :]

        state_t = state_scr[h]
        acc_scr[...] = _dot_nt(q * jnp.exp(bcum), state_t)
        q_scr[...] = q
        b_scr[...] = bcum

        for i in range(n_sub):
            r0 = i * SUB
            b_i = bcum[r0:r0 + SUB, :]
            k_i = kk[r0:r0 + SUB, :]
            v_i = v[r0:r0 + SUB, :]
            if i > 0:
                ref_row = bcum[r0:r0 + 1, :]
                q_in = q[r0:r0 + SUB, :] * jnp.exp(b_i - ref_row)
                k_in = kk[0:r0, :] * jnp.exp(ref_row - bcum[0:r0, :])
                p = _dot_nt(q_in, k_in)
                acc_scr[r0:r0 + SUB, :] += _dot(p, v[0:r0, :])
            rows = []
            for t in range(SUB):
                b_t = b_scr[r0 + t:r0 + t + 1, :]
                q_t = q_scr[r0 + t:r0 + t + 1, :]
                x = (q_t * k_i) * jnp.exp(jnp.minimum(b_t - b_i, 0.0))
                a = jnp.sum(x, axis=-1, keepdims=True)
                a = jnp.where(jrow <= t, a, 0.0)
                rows.append(jnp.sum(a * v_i, axis=0, keepdims=True))
            acc_scr[r0:r0 + SUB, :] += jnp.concatenate(rows, axis=0)

        k_out = kk * jnp.exp(b_last - bcum)
        state_scr[h] = state_t * jnp.exp(b_last) + _dot_tn(v, k_out)
        o_ref[:, pl.ds(off, HEAD_DIM)] = _gated_head_norm(acc_scr[...], z, gn_ref[...]).astype(o_ref.dtype)
        return carry

    lax.fori_loop(0, N_HEADS, head, 0)


def _hgrn2(proj, lb, gn_b, layer, batch):
    t = proj.shape[0]
    n_chunks = t // batch // CHUNK
    hw = N_HEADS * HEAD_DIM
    return pl.pallas_call(
        _hgrn2_kernel,
        name="hgrn2",
        grid=(batch, n_chunks),
        in_specs=[
            pl.BlockSpec((CHUNK, QKVZ_W), lambda b, n: (b * n_chunks + n, 1)),
            pl.BlockSpec((None, 1, hw), lambda b, n: (layer, 0, 0)),
            pl.BlockSpec((None, 1, HEAD_DIM), lambda b, n: (layer, 0, 0)),
        ],
        out_specs=pl.BlockSpec((CHUNK, hw), lambda b, n: (b * n_chunks + n, 0)),
        out_shape=jax.ShapeDtypeStruct((t, hw), BF16),
        scratch_shapes=[
            pltpu.VMEM((N_HEADS, HEAD_DIM, HEAD_DIM), F32),
            pltpu.VMEM((CHUNK, HEAD_DIM), F32),
            pltpu.VMEM((CHUNK, HEAD_DIM), F32),
            pltpu.VMEM((CHUNK, HEAD_DIM), F32),
        ],
        compiler_params=_params("parallel", "arbitrary"),
    )(proj, lb, gn_b)


def _memkv_kernel(m_ref, g_ref, w_ref, o_ref):
    h = _rms(m_ref[...], g_ref[...]).astype(BF16)
    o_ref[...] = jnp.dot(h, w_ref[...], preferred_element_type=F32).astype(o_ref.dtype)


def _memkv(mem2, g_mem, w_kv, layer, tn=512):
    m, d = mem2.shape
    n = w_kv.shape[-1]
    return pl.pallas_call(
        _memkv_kernel,
        name="memkv",
        grid=(n // tn,),
        in_specs=[
            pl.BlockSpec((m, d), lambda j: (0, 0)),
            pl.BlockSpec((None, 1, d), lambda j: (layer, 0, 0)),
            pl.BlockSpec((None, d, tn), lambda j: (layer, 0, j)),
        ],
        out_specs=pl.BlockSpec((m, tn), lambda j: (0, j)),
        out_shape=jax.ShapeDtypeStruct((m, n), BF16),
        compiler_params=_params("arbitrary"),
    )(mem2, g_mem, w_kv)


def _merge_kernel(oa_ref, ob_ref, q_ref, k_ref, v_ref, ga0, ga1, gb0, gb1, gc0, gc1,
                  wa_ref, wb_ref, wc_ref, o_ref):
    heads = []
    for h in range(MEM_HEADS):
        lo, hi = h * MEM_HEAD_DIM, (h + 1) * MEM_HEAD_DIM
        s = _dot_nt(q_ref[:, lo:hi], k_ref[:, lo:hi]) * (MEM_HEAD_DIM ** -0.5)
        p = jnp.exp(s - jnp.max(s, axis=-1, keepdims=True))
        p = p / jnp.sum(p, axis=-1, keepdims=True)
        heads.append(_dot(p, v_ref[:, lo:hi]))
    oc = jnp.concatenate(heads, axis=-1).astype(BF16)

    half = o_ref.shape[-1] // 2
    pa =jnp.dot(oa_ref[...], wa_ref[...], preferred_element_type=F32)
    pb = jnp.dot(ob_ref[...], wb_ref[...], preferred_element_type=F32)
    pc = jnp.dot(oc, wc_ref[...], preferred_element_type=F32)
    for part, (ra, rb, rc) in enumerate(((ga0, gb0, gc0), (ga1, gb1, gc1))):
        lo, hi = part * half, (part + 1) * half
        merged = (_sigmoid(ra[...]) * pa[:, lo:hi] + _sigmoid(rb[...]) * pb[:, lo:hi]
                  + _sigmoid(rc[...]) * pc[:, lo:hi])
        o_ref[:, lo:hi] = merged.astype(o_ref.dtype)


def _merge(oa, ob, proj, kv, w_a, w_b, w_c, layer, batch, n_mem, tm=256):
    t = oa.shape[0]
    d = w_a.shape[-1]
    cw = MEM_HEADS * MEM_HEAD_DIM
    half = d // 2
    tiles_per_batch = t // batch // tm
    q_blk = 2 * QKVZ_W // cw
    g_blk = (2 * QKVZ_W + cw) // half
    gate_specs = [pl.BlockSpec((tm, half), (lambda i, _j=j: (i, g_blk + _j))) for j in range(2 * N_BRANCH)]
    wspec = lambda: pl.BlockSpec((None, w_a.shape[1], d), lambda i: (layer, 0, 0))
    return pl.pallas_call(
        _merge_kernel,
        name="merge",
        grid=(t // tm,),
        in_specs=[
            pl.BlockSpec((tm, oa.shape[1]), lambda i: (i, 0)),
            pl.BlockSpec((tm, ob.shape[1]), lambda i: (i, 0)),
            pl.BlockSpec((tm, cw), lambda i: (i, q_blk)),
            pl.BlockSpec((n_mem, cw), lambda i: (i // tiles_per_batch, 0)),
            pl.BlockSpec((n_mem, cw), lambda i: (i // tiles_per_batch, 1)),
            *gate_specs,
            wspec(), wspec(), wspec(),
        ],
        out_specs=pl.BlockSpec((tm, d), lambda i: (i, 0)),
        out_shape=jax.ShapeDtypeStruct((t, d), BF16),
        compiler_params=_params("parallel"),
    )(oa, ob, proj, kv, kv, *([proj] * (2 * N_BRANCH)), w_a, w_b, w_c)


def _outproj_kernel(m_ref, w_ref, x_ref, g_ref, o_ref):
    y = jnp.dot(m_ref[...], w_ref[...], preferred_element_type=F32)
    o_ref[...] = x_ref[...] + _rms(y, g_ref[...])


def _outproj(merged, w_out, x2, g_post, layer, tm=512):
    t, d = x2.shape
    return pl.pallas_call(
        _outproj_kernel,
        name="outproj",
        grid=(t // tm,),
        in_specs=[
            pl.BlockSpec((tm, d), lambda i: (i, 0)),
            pl.BlockSpec((None, d, d), lambda i: (layer, 0, 0)),
            pl.BlockSpec((tm, d), lambda i: (i, 0)),
            pl.BlockSpec((None, 1, d), lambda i: (layer, 0, 0)),
        ],
        out_specs=pl.BlockSpec((tm, d), lambda i: (i, 0)),
        out_shape=jax.ShapeDtypeStruct((t, d), F32),
        compiler_params=_params("parallel"),
    )(merged, w_out, x2, g_post)


def _mlp_kernel(x_ref, gpre_ref, w1_ref, w2_ref, gpost_ref, o_ref, h_scr, acc_scr):
    j = pl.program_id(1)

    @pl.when(j == 0)
    def _():
        h_scr[...] = _rms(x_ref[...], gpre_ref[...]).astype(BF16)
        acc_scr[...] = jnp.zeros_like(acc_scr)

    a = jnp.dot(h_scr[...], w1_ref[...], preferred_element_type=F32)
    a = jnp.maximum(a, 0.0)
    acc_scr[...] += jnp.dot((a * a).astype(BF16), w2_ref[...], preferred_element_type=F32)

    @pl.when(j == pl.num_programs(1) - 1)
    def _():
        o_ref[...] = x_ref[...] + _rms(acc_scr[...], gpost_ref[...])


def _mlp(x2, g_pre, w1, w2, g_post, layer, tm=512, tf=512):
    t, d = x2.shape
    dff = w1.shape[-1]
    return pl.pallas_call(
        _mlp_kernel,
        name="mlp",
        grid=(t // tm, dff // tf),
        in_specs=[
            pl.BlockSpec((tm, d), lambda i, j: (i, 0)),
            pl.BlockSpec((None, 1, d), lambda i, j: (layer, 0, 0)),
            pl.BlockSpec((None, d, tf), lambda i, j: (layer, 0, j)),
            pl.BlockSpec((None, tf, d), lambda i, j: (layer, j, 0)),
            pl.BlockSpec((None, 1, d), lambda i, j: (layer, 0, 0)),
        ],
        out_specs=pl.BlockSpec((tm, d), lambda i, j: (i, 0)),
        out_shape=jax.ShapeDtypeStruct((t, d), F32),
        scratch_shapes=[pltpu.VMEM((tm, d), BF16), pltpu.VMEM((tm, d), F32)],
        compiler_params=_params("parallel", "arbitrary"),
    )(x2, g_pre, w1, w2, g_post)


def kernel(x, mem, g_pre_mix, w_in, conv_a, a_log, dt_bias, gn_a, lb_raw, gn_b, g_mem, w_kv_mem, w_br_a, w_br_b, w_br_c, w_out, g_post_mix, g_pre_mlp, w_mlp_in, w_mlp_out, g_post_mlp):
    batch, seq, d = x.shape
    depth = w_in.shape[0]
    n_mem = mem.shape[1]
    hw = N_HEADS * HEAD_DIM
    x2 = x.reshape(batch * seq, d)
    mem2 = mem.reshape(batch * n_mem, d)

    small0 = QKVZ_W
    small1 = QKVZ_W + 2 * N_HEADS
    w_main = jnp.concatenate([w_in[:, :, :small0], w_in[:, :, small1:]], axis=-1).astype(BF16)
    w_small = jnp.pad(w_in[:, :, small0:small1], ((0, 0), (0, 0), (0, SMALL_W - 2 * N_HEADS))).astype(BF16)
    pad_row = lambda p: jnp.pad(p, ((0, 0), (N_HEADS, SMALL_W - 2 * N_HEADS)))[:, None, :]
    alog_row = pad_row(a_log.astype(F32))
    dtb_row = pad_row(dt_bias.astype(F32))
    row = lambda p: p.astype(F32)[:, None, :]

    lb_all = jnp.cumsum(jax.nn.softmax(lb_raw.astype(F32), axis=0), axis=0)
    lb_all = (lb_all - lb_all[0])[:, None, :]

    w_kv = w_kv_mem.astype(BF16)
    w_a, w_b, w_c = w_br_a.astype(BF16), w_br_b.astype(BF16), w_br_c.astype(BF16)
    w_o = w_out.astype(BF16)
    w1, w2 = w_mlp_in.astype(BF16), w_mlp_out.astype(BF16)
    g_pre_mix, g_mem, g_post_mix, g_pre_mlp, g_post_mlp = map(row, (g_pre_mix, g_mem, g_post_mix, g_pre_mlp, g_post_mlp))
    gn_a, gn_b = row(gn_a), row(gn_b)
    conv_a = conv_a.astype(F32)

    for layer in range(depth):
        proj, small = _inproj(x2, g_pre_mix, w_main, w_small, layer)
        oa = _deltanet(proj, small, conv_a, alog_row, dtb_row, gn_a, layer, batch)
        ob = _hgrn2(proj, lb_all, gn_b, layer, batch)
        kv = _memkv(mem2, g_mem, w_kv, layer)
        merged = _merge(oa, ob, proj, kv, w_a, w_b, w_c, layer, batch, n_mem)
        x2 = _outproj(merged, w_o, x2, g_post_mix, layer)
        x2 = _mlp(x2, g_pre_mlp, w1, w2, g_post_mlp, layer)
    return x2.reshape(batch, seq, d)
```

```python
import functools

import jax
import jax.numpy as jnp
from jax import lax
from jax.experimental import pallas as pl
from jax.experimental.pallas import tpu as pltpu

F32 = jnp.float32
BF16 = jnp.bfloat16

EPS = 1e-6
CHUNK = 64
SUB = 16
HSUB = 8
HEAD_DIM = 128
N_HEADS = 8
CONV_TAPS = 4
MEM_HEADS = 4
MEM_HEAD_DIM = 256
N_BRANCH = 3
QKVZ_W = 4 * N_HEADS * HEAD_DIM
SMALL_W = 128
VMEM_LIMIT = 56 * 1024 * 1024


def _params(*sem):
    return pltpu.CompilerParams(dimension_semantics=sem, vmem_limit_bytes=VMEM_LIMIT)


def _sigmoid(x):
    return 1.0 / (1.0 + jnp.exp(-x))


def _silu(x):
    return x * _sigmoid(x)


def _rms(x, g):
    return x * lax.rsqrt(jnp.mean(x * x, axis=-1, keepdims=True) + EPS) * g


def _dot(a, b):
    return jnp.dot(a.astype(BF16), b.astype(BF16), preferred_element_type=F32)


def _dot_nt(a, b):
    return lax.dot_general(a.astype(BF16), b.astype(BF16), (((1,), (1,)), ((), ())),
                           preferred_element_type=F32)


def _dot_tn(a, b):
    return lax.dot_general(a.astype(BF16), b.astype(BF16), (((0,), (0,)), ((), ())),
                           preferred_element_type=F32)


def _split3(x):
    hi = x.astype(BF16)
    r1 = x - hi.astype(F32)
    mid = r1.astype(BF16)
    lo = (r1 - mid.astype(F32)).astype(BF16)
    return hi, mid, lo


def _dot_sel(sel, x):
    hi, mid, lo = _split3(x)
    d = lambda t: jnp.dot(sel, t, preferred_element_type=F32)
    return d(hi) + d(mid) + d(lo)


def _dot_hp(a, b):
    a_hi = a.astype(BF16)
    a_lo = (a - a_hi.astype(F32)).astype(BF16)
    b_hi = b.astype(BF16)
    b_lo = (b - b_hi.astype(F32)).astype(BF16)
    d = lambda s, t: jnp.dot(s, t, preferred_element_type=F32)
    return d(a_hi, b_hi) + (d(a_hi, b_lo) + d(a_lo, b_hi))


def _tril_ones(n):
    r = lax.broadcasted_iota(jnp.int32, (n, n), 0)
    c = lax.broadcasted_iota(jnp.int32, (n, n), 1)
    return (r >= c).astype(BF16)


def _inproj_kernel(x_ref, g_ref, w_ref, ws_ref, o_ref, os_ref, h_scr):
    @pl.when(pl.program_id(1) == 0)
    def _():
        h = _rms(x_ref[...], g_ref[...]).astype(BF16)
        h_scr[...] = h
        os_ref[...] = jnp.dot(h, ws_ref[...], preferred_element_type=F32)

    o_ref[...] = jnp.dot(h_scr[...], w_ref[...], preferred_element_type=F32)


def _inproj(x2, g, w_main, w_small, layer, tm=1024, tn=1024):
    t, d = x2.shape
    n = w_main.shape[-1]
    return pl.pallas_call(
        _inproj_kernel,
        name="inproj",
        grid=(t // tm, n // tn),
        in_specs=[
            pl.BlockSpec((tm, d), lambda i, j: (i, 0)),
            pl.BlockSpec((None, 1, d), lambda i, j: (layer, 0, 0)),
            pl.BlockSpec((None, d, tn), lambda i, j: (layer, 0, j)),
            pl.BlockSpec((None, d, SMALL_W), lambda i, j: (layer, 0, 0)),
        ],
        out_specs=[
            pl.BlockSpec((tm, tn), lambda i, j: (i, j)),
            pl.BlockSpec((tm, SMALL_W), lambda i, j: (i, 0)),
        ],
        out_shape=[
            jax.ShapeDtypeStruct((t, n), F32),
            jax.ShapeDtypeStruct((t, SMALL_W), F32),
        ],
        scratch_shapes=[pltpu.VMEM((tm, d), BF16)],
        compiler_params=_params("parallel", "arbitrary"),
    )(x2, g, w_main, w_small)


def _gated_head_norm(o, z, g):
    return o * lax.rsqrt(jnp.mean(o * o, axis=-1, keepdims=True) + EPS) * g * _silu(z)


def _unit_lower_inverses(lows, dot):
    n = lows[0].shape[0]
    r = lax.broadcasted_iota(jnp.int32, (n, n), 0)
    c = lax.broadcasted_iota(jnp.int32, (n, n), 1)
    eye = (r == c).astype(F32)
    same = lambda s: (r >> (s.bit_length() - 1)) == (c >> (s.bit_length() - 1))
    size = SUB
    blks = [jnp.where(same(size), low, 0.0) for low in lows]
    p2 = [dot(b, b) for b in blks]
    invs = [eye - b for b in blks]
    p4 = [dot(p, p) for p in p2]
    invs = [i + dot(i, p) for i, p in zip(invs, p2)]
    p8 = [dot(p, p) for p in p4]
    invs = [i + dot(i, p) for i, p in zip(invs, p4)]
    invs = [i + dot(i, p) for i, p in zip(invs, p8)]
    while size < n:
        offs = [jnp.where(same(2 * size), jnp.where(same(size), 0.0, low), 0.0) for low in lows]
        tmp = [dot(i, o) for i, o in zip(invs, offs)]
        invs = [i - dot(t, i) for i, t in zip(invs, tmp)]
        size *= 2
    return invs


def _deltanet_kernel(qkvz_ref, small_ref, conv_ref, alog_ref, dtb_ref, gn_ref, o_ref,
                     state_scr, ext_scr):
    hw = N_HEADS * HEAD_DIM
    heads = range(N_HEADS)
    cols = [(h * HEAD_DIM, (h + 1) * HEAD_DIM) for h in heads]

    @pl.when(pl.program_id(1) == 0)
    def _():
        state_scr[...] = jnp.zeros_like(state_scr)
        ext_scr[0:8, :] = jnp.zeros((8, 3 * hw), F32)

    ext_scr[8:8 + CHUNK, :] = qkvz_ref[:, 0:3 * hw]

    r = lax.broadcasted_iota(jnp.int32, (CHUNK, CHUNK), 0)
    c = lax.broadcasted_iota(jnp.int32, (CHUNK, CHUNK), 1)
    causal = r >= c
    strict = r > c

    sm = small_ref[...]
    beta_all = _sigmoid(sm)
    pre = sm + dtb_ref[...]
    softplus = jnp.maximum(pre, 0.0) + jnp.log(1.0 + jnp.exp(-jnp.abs(pre)))
    logdecay = -jnp.exp(alog_ref[...]) * softplus
    gam_all = _dot_sel(_tril_ones(CHUNK), logdecay)
    gam_t = jnp.concatenate([gam_all, jnp.zeros_like(gam_all)], axis=0).T

    def conv(sec, lo, hi):
        base = sec * hw
        acc = None
        for k in range(CONV_TAPS):
            row0 = 8 - (CONV_TAPS - 1) + k
            term = conv_ref[k:k + 1, base + lo:base + hi] * ext_scr[row0:row0 + CHUNK, base + lo:base + hi]
            acc = term if acc is None else acc + term
        return _silu(acc)

    qs = [conv(0, lo, hi) for lo, hi in cols]
    ks = [conv(1, lo, hi) for lo, hi in cols]
    vs = [conv(2, lo, hi) for lo, hi in cols]
    qs = [q * lax.rsqrt(jnp.sum(q * q, axis=-1, keepdims=True) + EPS) * (HEAD_DIM ** -0.5) for q in qs]
    ks = [k * lax.rsqrt(jnp.sum(k * k, axis=-1, keepdims=True) + EPS) for k in ks]

    betas = [beta_all[:, h:h + 1] for h in heads]
    gcols = [gam_all[:, N_HEADS + h:N_HEADS + h + 1] for h in heads]
    grows = [gam_t[N_HEADS + h:N_HEADS + h + 1, 0:CHUNK] for h in heads]
    egs = [jnp.exp(g) for g in gcols]
    decs = [jnp.exp(jnp.where(causal, gc - gr, -jnp.inf)) for gc, gr in zip(gcols, grows)]
    kbs = [k * b for k, b in zip(ks, betas)]
    lows = [jnp.where(strict, _dot_nt(kb, k) * d, 0.0) for kb, k, d in zip(kbs, ks, decs)]
    a_qks = [_dot_nt(q, k) * d for q, k, d in zip(qs, ks, decs)]
    invs = _unit_lower_inverses(lows, _dot_hp)
    sols = [_dot(i, jnp.concatenate([v * b, kb * eg], axis=-1))
            for i, v, b, kb, eg in zip(invs, vs, betas, kbs, egs)]

    states = [state_scr[h] for h in heads]
    u_news = [sol[:, :HEAD_DIM] - _dot(sol[:, HEAD_DIM:], st) for sol, st in zip(sols, states)]
    outs = [_dot(q * eg, st) + _dot(a, un) for q, eg, st, a, un in zip(qs, egs, states, a_qks, u_news)]
    for h in heads:
        k_dec = ks[h] * jnp.exp(gcols[h][CHUNK - 1:CHUNK, :] - gcols[h])
        state_scr[h] = egs[h][CHUNK - 1:CHUNK, :] * states[h] + _dot_tn(k_dec, u_news[h])
    for h, (lo, hi) in enumerate(cols):
        z = qkvz_ref[:, 3 * hw + lo:3 * hw + hi]
        o_ref[:, lo:hi] = _gated_head_norm(outs[h], z, gn_ref[...]).astype(o_ref.dtype)

    ext_scr[0:8, :] = ext_scr[CHUNK:CHUNK + 8, :]


def _deltanet(proj, small, conv_a, alog_row, dtb_row, gn_a, layer, batch):
    t = proj.shape[0]
    n_chunks = t // batch // CHUNK
    hw = N_HEADS * HEAD_DIM
    return pl.pallas_call(
        _deltanet_kernel,
        name="deltanet",
        grid=(batch, n_chunks),
        in_specs=[
            pl.BlockSpec((CHUNK, QKVZ_W), lambda b, n: (b * n_chunks + n, 0)),
            pl.BlockSpec((CHUNK, SMALL_W), lambda b, n: (b * n_chunks + n, 0)),
            pl.BlockSpec((None, CONV_TAPS, 3 * hw), lambda b, n: (layer, 0, 0)),
            pl.BlockSpec((None, 1, SMALL_W), lambda b, n: (layer, 0, 0)),
            pl.BlockSpec((None, 1, SMALL_W), lambda b, n: (layer, 0, 0)),
            pl.BlockSpec((None, 1, HEAD_DIM), lambda b, n: (layer, 0, 0)),
        ],
        out_specs=pl.BlockSpec((CHUNK, hw), lambda b, n: (b * n_chunks + n, 0)),
        out_shape=jax.ShapeDtypeStruct((t, hw), BF16),
        scratch_shapes=[
            pltpu.VMEM((N_HEADS, HEAD_DIM, HEAD_DIM), F32),
            pltpu.VMEM((CHUNK + 8, 3 * hw), F32),
        ],
        compiler_params=_params("parallel", "arbitrary"),
    )(proj, small, conv_a, alog_row, dtb_row, gn_a)


def _hgrn2_kernel(qfiz_ref, lb_ref, gn_ref, o_ref, state_scr, q_scr, b_scr):
    hw = N_HEADS * HEAD_DIM
    heads = range(N_HEADS)
    cols = [(h * HEAD_DIM, (h + 1) * HEAD_DIM) for h in heads]
    n_sub = CHUNK // HSUB

    @pl.when(pl.program_id(1) == 0)
    def _():
        state_scr[...] = jnp.zeros_like(state_scr)

    tril = _tril_ones(CHUNK)
    lane = lax.broadcasted_iota(jnp.int32, (HSUB, HEAD_DIM), 1)
    subl = lax.broadcasted_iota(jnp.int32, (HSUB, HEAD_DIM), 0)
    r = lax.broadcasted_iota(jnp.int32, (CHUNK, CHUNK), 0)
    c = lax.broadcasted_iota(jnp.int32, (CHUNK, CHUNK), 1)
    same_block = (r >> (HSUB.bit_length() - 1)) == (c >> (HSUB.bit_length() - 1))

    lbs = [lb_ref[:, lo:hi] for lo, hi in cols]
    fs = [qfiz_ref[:, hw + lo:hw + hi] for lo, hi in cols]
    vs = [qfiz_ref[:, 2 * hw + lo:2 * hw + hi] for lo, hi in cols]
    qs = [_silu(qfiz_ref[:, lo:hi]) * (HEAD_DIM ** -0.5) for lo, hi in cols]
    log_sigs = [jnp.minimum(f, 0.0) - jnp.log(1.0 + jnp.exp(-jnp.abs(f))) for f in fs]
    las = [jnp.log(lb) for lb in lbs]
    l1s = [jnp.log(1.0 - lb) for lb in lbs]
    lcs = [l1 + ls for l1, ls in zip(l1s, log_sigs)]
    log_fs = [jnp.maximum(la, lc) + jnp.log(1.0 + jnp.exp(-jnp.abs(la - lc))) for la, lc in zip(las, lcs)]
    log_ks = [lc - f for lc, f in zip(lcs, fs)]
    kks = [jnp.exp(lk) for lk in log_ks]
    bcums = [_dot_sel(tril, lf) for lf in log_fs]
    bcs = [b - lk for b, lk in zip(bcums, log_ks)]
    states = [state_scr[h] for h in heads]
    inters = [_dot_nt(q * jnp.exp(b), st) for q, b, st in zip(qs, bcums, states)]
    for h in heads:
        q_scr[h] = qs[h]
        b_scr[h] = bcums[h]

    a_offs = []
    for h in heads:
        blocks = [jnp.zeros((HSUB, CHUNK), F32)]
        for i in range(1, n_sub):
            r0 = i * HSUB
            ref_row = bcums[h][r0:r0 + 1, :]
            q_in = qs[h][r0:r0 + HSUB, :] * jnp.exp(bcums[h][r0:r0 + HSUB, :] - ref_row)
            k_in = kks[h][0:r0, :] * jnp.exp(ref_row - bcums[h][0:r0, :])
            p = _dot_nt(q_in, k_in)
            blocks.append(jnp.concatenate([p, jnp.zeros((HSUB, CHUNK - r0), F32)], axis=1))
        a_offs.append(jnp.concatenate(blocks, axis=0))
    offs = [_dot(a, v) for a, v in zip(a_offs, vs)]

    diags = []
    for h in heads:
        a_t = jnp.zeros((HSUB, HEAD_DIM), F32)
        for i in range(n_sub):
            r0 = i * HSUB
            bc_i = bcs[h][r0:r0 + HSUB, :]
            for t in range(r0, r0 + HSUB):
                x = q_scr[h, t:t + 1, :] * jnp.exp(jnp.minimum(b_scr[h, t:t + 1, :] - bc_i, 0.0))
                a_t = jnp.where(lane == t, jnp.sum(x, axis=-1, keepdims=True), a_t)
        a_t = jnp.where(subl <= (lane & (HSUB - 1)), a_t, 0.0)
        a_full = jnp.where(same_block, jnp.concatenate([a_t[:, :CHUNK]] * n_sub, axis=0), 0.0)
        diags.append(_dot_tn(a_full, vs[h]))

    for h, (lo, hi) in enumerate(cols):
        b_last = bcums[h][CHUNK - 1:CHUNK, :]
        k_out = kks[h] * jnp.exp(b_last - bcums[h])
        state_scr[h] = states[h] * jnp.exp(b_last) + _dot_tn(vs[h], k_out)
        z = qfiz_ref[:, 3 * hw + lo:3 * hw + hi]
        o = inters[h] + offs[h] + diags[h]
        o_ref[:, lo:hi] = _gated_head_norm(o, z, gn_ref[...]).astype(o_ref.dtype)


def _hgrn2(proj, lb, gn_b, layer, batch):
    t = proj.shape[0]
    n_chunks = t // batch // CHUNK
    hw = N_HEADS * HEAD_DIM
    return pl.pallas_call(
        _hgrn2_kernel,
        name="hgrn2",
        grid=(batch, n_chunks),
        in_specs=[
            pl.BlockSpec((CHUNK, QKVZ_W), lambda b, n: (b * n_chunks + n, 1)),
            pl.BlockSpec((None, 1, hw), lambda b, n: (layer, 0, 0)),
            pl.BlockSpec((None, 1, HEAD_DIM), lambda b, n: (layer, 0, 0)),
        ],
        out_specs=pl.BlockSpec((CHUNK, hw), lambda b, n: (b * n_chunks + n, 0)),
        out_shape=jax.ShapeDtypeStruct((t, hw), BF16),
        scratch_shapes=[
            pltpu.VMEM((N_HEADS, HEAD_DIM, HEAD_DIM), F32),
            pltpu.VMEM((N_HEADS, CHUNK, HEAD_DIM), F32),
            pltpu.VMEM((N_HEADS, CHUNK, HEAD_DIM), F32),
        ],
        compiler_params=_params("parallel", "arbitrary"),
    )(proj, lb, gn_b)


def _memkv_kernel(m_ref, g_ref, w_ref, o_ref):
    h = _rms(m_ref[...], g_ref[...]).astype(BF16)
    o_ref[...] = jnp.dot(h, w_ref[...], preferred_element_type=F32).astype(o_ref.dtype)


def _memkv(mem2, g_mem, w_kv, layer, tn=512):
    m, d = mem2.shape
    n = w_kv.shape[-1]
    return pl.pallas_call(
        _memkv_kernel,
        name="memkv",
        grid=(n // tn,),
        in_specs=[
            pl.BlockSpec((m, d), lambda j: (0, 0)),
            pl.BlockSpec((None, 1, d), lambda j: (layer, 0, 0)),
            pl.BlockSpec((None, d, tn), lambda j: (layer, 0, j)),
        ],
        out_specs=pl.BlockSpec((m, tn), lambda j: (0, j)),
        out_shape=jax.ShapeDtypeStruct((m, n), BF16),
        compiler_params=_params("arbitrary"),
    )(mem2, g_mem, w_kv)


def _merge_kernel(oa_ref, ob_ref, q_ref, k_ref, v_ref, ga0, ga1, gb0, gb1, gc0, gc1,
                  wa_ref, wb_ref, wc_ref, o_ref):
    heads = []
    for h in range(MEM_HEADS):
        lo, hi = h * MEM_HEAD_DIM, (h + 1) * MEM_HEAD_DIM
        s = _dot_nt(q_ref[:, lo:hi], k_ref[:, lo:hi]) * (MEM_HEAD_DIM ** -0.5)
        p = jnp.exp(s - jnp.max(s, axis=-1, keepdims=True))
        p = p / jnp.sum(p, axis=-1, keepdims=True)
        heads.append(_dot(p, v_ref[:, lo:hi]))
    oc = jnp.concatenate(heads, axis=-1).astype(BF16)

    half = o_ref.shape[-1] // 2
    pa =jnp.dot(oa_ref[...], wa_ref[...], preferred_element_type=F32)
    pb = jnp.dot(ob_ref[...], wb_ref[...], preferred_element_type=F32)
    pc = jnp.dot(oc, wc_ref[...], preferred_element_type=F32)
    for part, (ra, rb, rc) in enumerate(((ga0, gb0, gc0), (ga1, gb1, gc1))):
        lo, hi = part * half, (part + 1) * half
        merged = (_sigmoid(ra[...]) * pa[:, lo:hi] + _sigmoid(rb[...]) * pb[:, lo:hi]
                  + _sigmoid(rc[...]) * pc[:, lo:hi])
        o_ref[:, lo:hi] = merged.astype(o_ref.dtype)


def _merge(oa, ob, proj, kv, w_a, w_b, w_c, layer, batch, n_mem, tm=256):
    t = oa.shape[0]
    d = w_a.shape[-1]
    cw = MEM_HEADS * MEM_HEAD_DIM
    half = d // 2
    tiles_per_batch = t // batch // tm
    q_blk = 2 * QKVZ_W // cw
    g_blk = (2 * QKVZ_W + cw) // half
    gate_specs = [pl.BlockSpec((tm, half), (lambda i, _j=j: (i, g_blk + _j))) for j in range(2 * N_BRANCH)]
    wspec = lambda: pl.BlockSpec((None, w_a.shape[1], d), lambda i: (layer, 0, 0))
    return pl.pallas_call(
        _merge_kernel,
        name="merge",
        grid=(t // tm,),
        in_specs=[
            pl.BlockSpec((tm, oa.shape[1]), lambda i: (i, 0)),
            pl.BlockSpec((tm, ob.shape[1]), lambda i: (i, 0)),
            pl.BlockSpec((tm, cw), lambda i: (i, q_blk)),
            pl.BlockSpec((n_mem, cw), lambda i: (i // tiles_per_batch, 0)),
            pl.BlockSpec((n_mem, cw), lambda i: (i // tiles_per_batch, 1)),
            *gate_specs,
            wspec(), wspec(), wspec(),
        ],
        out_specs=pl.BlockSpec((tm, d), lambda i: (i, 0)),
        out_shape=jax.ShapeDtypeStruct((t, d), BF16),
        compiler_params=_params("parallel"),
    )(oa, ob, proj, kv, kv, *([proj] * (2 * N_BRANCH)), w_a, w_b, w_c)


def _outproj_kernel(m_ref, w_ref, x_ref, g_ref, o_ref):
    y = jnp.dot(m_ref[...], w_ref[...], preferred_element_type=F32)
    o_ref[...] = x_ref[...] + _rms(y, g_ref[...])


def _outproj(merged, w_out, x2, g_post, layer, tm=512):
    t, d = x2.shape
    return pl.pallas_call(
        _outproj_kernel,
        name="outproj",
        grid=(t // tm,),
        in_specs=[
            pl.BlockSpec((tm, d), lambda i: (i, 0)),
            pl.BlockSpec((None, d, d), lambda i: (layer, 0, 0)),
            pl.BlockSpec((tm, d), lambda i: (i, 0)),
            pl.BlockSpec((None, 1, d), lambda i: (layer, 0, 0)),
        ],
        out_specs=pl.BlockSpec((tm, d), lambda i: (i, 0)),
        out_shape=jax.ShapeDtypeStruct((t, d), F32),
        compiler_params=_params("parallel"),
    )(merged, w_out, x2, g_post)


def _mlp_kernel(x_ref, gpre_ref, w1_ref, w2_ref, gpost_ref, o_ref, h_scr, acc_scr):
    j = pl.program_id(1)

    @pl.when(j == 0)
    def _():
        h_scr[...] = _rms(x_ref[...], gpre_ref[...]).astype(BF16)
        acc_scr[...] = jnp.zeros_like(acc_scr)

    a = jnp.dot(h_scr[...], w1_ref[...], preferred_element_type=F32)
    a = jnp.maximum(a, 0.0)
    acc_scr[...] += jnp.dot((a * a).astype(BF16), w2_ref[...], preferred_element_type=F32)

    @pl.when(j == pl.num_programs(1) - 1)
    def _():
        o_ref[...] = x_ref[...] + _rms(acc_scr[...], gpost_ref[...])


def _mlp(x2, g_pre, w1, w2, g_post, layer, tm=512, tf=512):
    t, d = x2.shape
    dff = w1.shape[-1]
    return pl.pallas_call(
        _mlp_kernel,
        name="mlp",
        grid=(t // tm, dff // tf),
        in_specs=[
            pl.BlockSpec((tm, d), lambda i, j: (i, 0)),
            pl.BlockSpec((None, 1, d), lambda i, j: (layer, 0, 0)),
            pl.BlockSpec((None, d, tf), lambda i, j: (layer, 0, j)),
            pl.BlockSpec((None, tf, d), lambda i, j: (layer, j, 0)),
            pl.BlockSpec((None, 1, d), lambda i, j: (layer, 0, 0)),
        ],
        out_specs=pl.BlockSpec((tm, d), lambda i, j: (i, 0)),
        out_shape=jax.ShapeDtypeStruct((t, d), F32),
        scratch_shapes=[pltpu.VMEM((tm, d), BF16), pltpu.VMEM((tm, d), F32)],
        compiler_params=_params("parallel", "arbitrary"),
    )(x2, g_pre, w1, w2, g_post)


def kernel(x, mem, g_pre_mix, w_in, conv_a, a_log, dt_bias, gn_a, lb_raw, gn_b, g_mem, w_kv_mem, w_br_a, w_br_b, w_br_c, w_out, g_post_mix, g_pre_mlp, w_mlp_in, w_mlp_out, g_post_mlp):
    batch, seq, d = x.shape
    depth = w_in.shape[0]
    n_mem = mem.shape[1]
    hw = N_HEADS * HEAD_DIM
    x2 = x.reshape(batch * seq, d)
    mem2 = mem.reshape(batch * n_mem, d)

    small0 = QKVZ_W
    small1 = QKVZ_W + 2 * N_HEADS
    w_main = jnp.concatenate([w_in[:, :, :small0], w_in[:, :, small1:]], axis=-1).astype(BF16)
    w_small = jnp.pad(w_in[:, :, small0:small1], ((0, 0), (0, 0), (0, SMALL_W - 2 * N_HEADS))).astype(BF16)
    pad_row = lambda p: jnp.pad(p, ((0, 0), (N_HEADS, SMALL_W - 2 * N_HEADS)))[:, None, :]
    alog_row = pad_row(a_log.astype(F32))
    dtb_row = pad_row(dt_bias.astype(F32))
    row = lambda p: p.astype(F32)[:, None, :]

    lb_all = jnp.cumsum(jax.nn.softmax(lb_raw.astype(F32), axis=0), axis=0)
    lb_all = (lb_all - lb_all[0])[:, None, :]

    w_kv = w_kv_mem.astype(BF16)
    w_a, w_b, w_c = w_br_a.astype(BF16), w_br_b.astype(BF16), w_br_c.astype(BF16)
    w_o = w_out.astype(BF16)
    w1, w2 = w_mlp_in.astype(BF16), w_mlp_out.astype(BF16)
    g_pre_mix, g_mem, g_post_mix, g_pre_mlp, g_post_mlp = map(row, (g_pre_mix, g_mem, g_post_mix, g_pre_mlp, g_post_mlp))
    gn_a, gn_b = row(gn_a), row(gn_b)
    conv_a = conv_a.astype(F32)

    for layer in range(depth):
        proj, small = _inproj(x2, g_pre_mix, w_main, w_small, layer)
        oa = _deltanet(proj, small, conv_a, alog_row, dtb_row, gn_a, layer, batch)
        ob = _hgrn2(proj, lb_all, gn_b, layer, batch)
        kv = _memkv(mem2, g_mem, w_kv, layer)
        merged = _merge(oa, ob, proj, kv, w_a, w_b, w_c, layer, batch, n_mem)
        x2 = _outproj(merged, w_o, x2, g_post_mix, layer)
        x2 = _mlp(x2, g_pre_mlp, w1, w2, g_post_mlp, layer)
    return x2.reshape(batch, seq, d)
```

```python
import jax
import jax.numpy as jnp
from jax import lax
from jax.experimental import pallas as pl
from jax.experimental.pallas import tpu as pltpu

F32 = jnp.float32
BF16 = jnp.bfloat16

EPS = 1e-6
CHUNK = 64
SUB = 4
HSUB = 8
HEAD_DIM = 128
N_HEADS = 8
CONV_TAPS = 4
MEM_HEADS = 4
MEM_HEAD_DIM = 256
N_BRANCH = 3
QKVZ_W = 4 * N_HEADS * HEAD_DIM
SMALL_W = 128
VMEM_LIMIT = 56 * 1024 * 1024


def _params(*sem):
    return pltpu.CompilerParams(dimension_semantics=sem, vmem_limit_bytes=VMEM_LIMIT)


def _sigmoid(x):
    return 1.0 / (1.0 + jnp.exp(-x))


def _silu(x):
    return x * _sigmoid(x)


def _rms(x, g):
    return x * lax.rsqrt(jnp.mean(x * x, axis=-1, keepdims=True) + EPS) * g


def _dot(a, b):
    return jnp.dot(a.astype(BF16), b.astype(BF16), preferred_element_type=F32)


def _dot_nt(a, b):
    return lax.dot_general(a.astype(BF16), b.astype(BF16), (((1,), (1,)), ((), ())),
                           preferred_element_type=F32)


def _dot_tn(a, b):
    return lax.dot_general(a.astype(BF16), b.astype(BF16), (((0,), (0,)), ((), ())),
                           preferred_element_type=F32)


def _split3(x):
    hi = x.astype(BF16)
    r1 = x - hi.astype(F32)
    mid = r1.astype(BF16)
    lo = (r1 - mid.astype(F32)).astype(BF16)
    return hi, mid, lo


def _dot_sel(sel, x):
    hi, mid, lo = _split3(x)
    d = lambda t: jnp.dot(sel, t, preferred_element_type=F32)
    return d(hi) + d(mid) + d(lo)


def _tril_ones(n):
    r = lax.broadcasted_iota(jnp.int32, (n, n), 0)
    c = lax.broadcasted_iota(jnp.int32, (n, n), 1)
    return (r >= c).astype(BF16)


def _prenorm_kernel(x_ref, g_ref, o_ref):
    o_ref[...] = _rms(x_ref[...], g_ref[...]).astype(o_ref.dtype)


def _prenorm(x2, g, layer, tm=1024):
    t, d = x2.shape
    return pl.pallas_call(
        _prenorm_kernel,
        name="prenorm",
        grid=(t // tm,),
        in_specs=[
            pl.BlockSpec((tm, d), lambda i: (i, 0)),
            pl.BlockSpec((None, 1, d), lambda i: (layer, 0, 0)),
        ],
        out_specs=pl.BlockSpec((tm, d), lambda i: (i, 0)),
        out_shape=jax.ShapeDtypeStruct((t, d), BF16),
        compiler_params=_params("parallel"),
    )(x2, g)


def _inproj_kernel(h_ref, w_ref, ws_ref, o_ref, os_ref):
    @pl.when(pl.program_id(1) == 0)
    def _():
        os_ref[...] = jnp.dot(h_ref[...], ws_ref[...], preferred_element_type=F32)

    o_ref[...] = jnp.dot(h_ref[...], w_ref[...], preferred_element_type=F32)


def _inproj(h, w_main, w_small, layer, tm=2048, tn=512):
    t, d = h.shape
    n = w_main.shape[-1]
    tm = min(tm, t)
    return pl.pallas_call(
        _inproj_kernel,
        name="inproj",
        grid=(t // tm, n // tn),
        in_specs=[
            pl.BlockSpec((tm, d), lambda i, j: (i, 0)),
            pl.BlockSpec((None, d, tn), lambda i, j: (layer, 0, j)),
            pl.BlockSpec((None, d, SMALL_W), lambda i, j: (layer, 0, 0)),
        ],
        out_specs=[
            pl.BlockSpec((tm, tn), lambda i, j: (i, j)),
            pl.BlockSpec((tm, SMALL_W), lambda i, j: (i, 0)),
        ],
        out_shape=[
            jax.ShapeDtypeStruct((t, n), F32),
            jax.ShapeDtypeStruct((t, SMALL_W), F32),
        ],
        compiler_params=_params("parallel", "arbitrary"),
    )(h, w_main, w_small)


def _gated_head_norm(o, z, g):
    return o * lax.rsqrt(jnp.mean(o * o, axis=-1, keepdims=True) + EPS) * g * _silu(z)


def _unit_lower_inverses(lows, dot):
    n = lows[0].shape[0]
    r = lax.broadcasted_iota(jnp.int32, (n, n), 0)
    c = lax.broadcasted_iota(jnp.int32, (n, n), 1)
    eye = (r == c).astype(F32)
    same = lambda s: (r >> (s.bit_length() - 1)) == (c >> (s.bit_length() - 1))
    size = SUB
    pows = [jnp.where(same(size), low, 0.0) for low in lows]
    invs = [eye - b for b in pows]
    order = 2
    while order < SUB:
        pows = [dot(p, p) for p in pows]
        invs = [i + dot(i, p) for i, p in zip(invs, pows)]
        order *= 2
    while size < n:
        offs = [jnp.where(same(2 * size), jnp.where(same(size), 0.0, low), 0.0) for low in lows]
        tmp = [dot(i, o) for i, o in zip(invs, offs)]
        invs = [i - dot(t, i) for i, t in zip(invs, tmp)]
        size *= 2
    return invs


def _deltanet_kernel(qkvz_ref, small_ref, conv_ref, alog_ref, dtb_ref, gn_ref, o_ref,
                     state_scr, ext_scr):
    hw = N_HEADS * HEAD_DIM
    nb = qkvz_ref.shape[0]
    chains = [(bi, h) for bi in range(nb) for h in range(N_HEADS)]

    @pl.when(pl.program_id(1) == 0)
    def _():
        state_scr[...] = jnp.zeros_like(state_scr)
        ext_scr[:, 0:8, :] = jnp.zeros((nb, 8, 3 * hw), F32)

    for bi in range(nb):
        ext_scr[bi, 8:8 + CHUNK, :] = qkvz_ref[bi, :, 0:3 * hw]

    r = lax.broadcasted_iota(jnp.int32, (CHUNK, CHUNK), 0)
    c = lax.broadcasted_iota(jnp.int32, (CHUNK, CHUNK), 1)
    causal = r >= c
    strict = r > c
    tril = _tril_ones(CHUNK)

    beta_all, gam_all, gam_t = [], [], []
    for bi in range(nb):
        sm = small_ref[bi]
        beta_all.append(_sigmoid(sm))
        pre = sm + dtb_ref[...]
        softplus = jnp.maximum(pre, 0.0) + jnp.log(1.0 + jnp.exp(-jnp.abs(pre)))
        logdecay = -jnp.exp(alog_ref[...]) * softplus
        gam = _dot_sel(tril, logdecay)
        gam_all.append(gam)
        gam_t.append(jnp.concatenate([gam, jnp.zeros_like(gam)], axis=0).T)

    def conv(sec, bi, h):
        lo = sec * hw + h * HEAD_DIM
        acc = None
        for k in range(CONV_TAPS):
            row0 = 8 - (CONV_TAPS - 1) + k
            term = conv_ref[k:k + 1, lo:lo + HEAD_DIM] * ext_scr[bi, row0:row0 + CHUNK, lo:lo + HEAD_DIM]
            acc = term if acc is None else acc + term
        return _silu(acc)

    qs = [conv(0, bi, h) for bi, h in chains]
    ks = [conv(1, bi, h) for bi, h in chains]
    vs = [conv(2, bi, h) for bi, h in chains]
    qs = [q * lax.rsqrt(jnp.sum(q * q, axis=-1, keepdims=True) + EPS) * (HEAD_DIM ** -0.5) for q in qs]
    ks = [k * lax.rsqrt(jnp.sum(k * k, axis=-1, keepdims=True) + EPS) for k in ks]

    betas = [beta_all[bi][:, h:h + 1] for bi, h in chains]
    gcols = [gam_all[bi][:, N_HEADS + h:N_HEADS + h + 1] for bi, h in chains]
    grows = [gam_t[bi][N_HEADS + h:N_HEADS + h + 1, 0:CHUNK] for bi, h in chains]
    egs = [jnp.exp(g) for g in gcols]
    decs = [jnp.exp(jnp.where(causal, gc - gr, -jnp.inf)) for gc, gr in zip(gcols, grows)]
    kbs = [k * b for k, b in zip(ks, betas)]
    lows = [jnp.where(strict, _dot_nt(kb, k) * d, 0.0) for kb, k, d in zip(kbs, ks, decs)]
    a_qks = [_dot_nt(q, k) * d for q, k, d in zip(qs, ks, decs)]
    invs = _unit_lower_inverses(lows, _dot)
    sols = [_dot(i, jnp.concatenate([v * b, kb * eg], axis=-1))
            for i, v, b, kb, eg in zip(invs, vs, betas, kbs, egs)]

    states = [state_scr[n] for n in range(len(chains))]
    u_news = [sol[:, :HEAD_DIM] - _dot(sol[:, HEAD_DIM:], st) for sol, st in zip(sols, states)]
    outs = [_dot(q * eg, st) + _dot(a, un) for q, eg, st, a, un in zip(qs, egs, states, a_qks, u_news)]
    for n in range(len(chains)):
        k_dec = ks[n] * jnp.exp(gcols[n][CHUNK - 1:CHUNK, :] - gcols[n])
        state_scr[n] = egs[n][CHUNK - 1:CHUNK, :] * states[n] + _dot_tn(k_dec, u_news[n])
    for n, (bi, h) in enumerate(chains):
        lo = h * HEAD_DIM
        z = qkvz_ref[bi, :, 3 * hw + lo:3 * hw + lo + HEAD_DIM]
        o_ref[bi, :, lo:lo + HEAD_DIM] = _gated_head_norm(outs[n], z, gn_ref[...]).astype(o_ref.dtype)

    for bi in range(nb):
        ext_scr[bi, 0:8, :] = ext_scr[bi, CHUNK:CHUNK + 8, :]


def _deltanet(proj, small, conv_a, alog_row, dtb_row, gn_a, layer, nb=2):
    batch, seq, _ = proj.shape
    hw = N_HEADS * HEAD_DIM
    return pl.pallas_call(
        _deltanet_kernel,
        name="deltanet",
        grid=(batch // nb, seq // CHUNK),
        in_specs=[
            pl.BlockSpec((nb, CHUNK, QKVZ_W), lambda b, n: (b, n, 0)),
            pl.BlockSpec((nb, CHUNK, SMALL_W), lambda b, n: (b, n, 0)),
            pl.BlockSpec((None, CONV_TAPS, 3 * hw), lambda b, n: (layer, 0, 0)),
            pl.BlockSpec((None, 1, SMALL_W), lambda b, n: (layer, 0, 0)),
            pl.BlockSpec((None, 1, SMALL_W), lambda b, n: (layer, 0, 0)),
            pl.BlockSpec((None, 1, HEAD_DIM), lambda b, n: (layer, 0, 0)),
        ],
        out_specs=pl.BlockSpec((nb, CHUNK, hw), lambda b, n: (b, n, 0)),
        out_shape=jax.ShapeDtypeStruct((batch, seq, hw), BF16),
        scratch_shapes=[
            pltpu.VMEM((nb * N_HEADS, HEAD_DIM, HEAD_DIM), F32),
            pltpu.VMEM((nb, CHUNK + 8, 3 * hw), F32),
        ],
        compiler_params=_params("parallel", "arbitrary"),
    )(proj, small, conv_a, alog_row, dtb_row, gn_a)


def _hgrn2_kernel(qfiz_ref, lb_ref, gn_ref, o_ref, state_scr, q_scr, b_scr):
    hw = N_HEADS * HEAD_DIM
    nb = qfiz_ref.shape[0]
    chains = [(bi, h) for bi in range(nb) for h in range(N_HEADS)]
    heads = range(len(chains))
    col = lambda sec, h: slice(sec * hw + h * HEAD_DIM, sec * hw + (h + 1) * HEAD_DIM)
    n_sub = CHUNK // HSUB

    @pl.when(pl.program_id(1) == 0)
    def _():
        state_scr[...] = jnp.zeros_like(state_scr)

    tril = _tril_ones(CHUNK)
    lane = lax.broadcasted_iota(jnp.int32, (HSUB, HEAD_DIM), 1)
    subl = lax.broadcasted_iota(jnp.int32, (HSUB, HEAD_DIM), 0)
    r = lax.broadcasted_iota(jnp.int32, (CHUNK, CHUNK), 0)
    c = lax.broadcasted_iota(jnp.int32, (CHUNK, CHUNK), 1)
    same_block = (r >> (HSUB.bit_length() - 1)) == (c >> (HSUB.bit_length() - 1))

    lbs = [lb_ref[:, col(0, h)] for _, h in chains]
    fs = [qfiz_ref[bi, :, col(1, h)] for bi, h in chains]
    vs = [qfiz_ref[bi, :, col(2, h)] for bi, h in chains]
    qs = [_silu(qfiz_ref[bi, :, col(0, h)]) * (HEAD_DIM ** -0.5) for bi, h in chains]
    log_sigs = [jnp.minimum(f, 0.0) - jnp.log(1.0 + jnp.exp(-jnp.abs(f))) for f in fs]
    las = [jnp.log(lb) for lb in lbs]
    l1s = [jnp.log(1.0 - lb) for lb in lbs]
    lcs = [l1 + ls for l1, ls in zip(l1s, log_sigs)]
    log_fs = [jnp.maximum(la, lc) + jnp.log(1.0 + jnp.exp(-jnp.abs(la - lc))) for la, lc in zip(las, lcs)]
    log_ks = [lc - f for lc, f in zip(lcs, fs)]
    kks = [jnp.exp(lk) for lk in log_ks]
    bcums = [_dot_sel(tril, lf) for lf in log_fs]
    bcs = [b - lk for b, lk in zip(bcums, log_ks)]
    states = [state_scr[h] for h in heads]
    inters = [_dot_nt(q * jnp.exp(b), st) for q, b, st in zip(qs, bcums, states)]
    for h in heads:
        q_scr[h] = qs[h]
        b_scr[h] = bcums[h]

    a_offs = []
    for h in heads:
        blocks = [jnp.zeros((HSUB, CHUNK), F32)]
        for i in range(1, n_sub):
            r0 = i * HSUB
            ref_row = bcums[h][r0:r0 + 1, :]
            q_in = qs[h][r0:r0 + HSUB, :] * jnp.exp(bcums[h][r0:r0 + HSUB, :] - ref_row)
            k_in = kks[h][0:r0, :] * jnp.exp(ref_row - bcums[h][0:r0, :])
            p = _dot_nt(q_in, k_in)
            blocks.append(jnp.concatenate([p, jnp.zeros((HSUB, CHUNK - r0), F32)], axis=1))
        a_offs.append(jnp.concatenate(blocks, axis=0))
    offs = [_dot(a, v) for a, v in zip(a_offs, vs)]

    diags = []
    for h in heads:
        a_t = jnp.zeros((HSUB, HEAD_DIM), F32)
        for i in range(n_sub):
            r0 = i * HSUB
            bc_i = bcs[h][r0:r0 + HSUB, :]
            for t in range(r0, r0 + HSUB):
                x = q_scr[h, t:t + 1, :] * jnp.exp(jnp.minimum(b_scr[h, t:t + 1, :] - bc_i, 0.0))
                a_t = jnp.where(lane == t, jnp.sum(x, axis=-1, keepdims=True), a_t)
        a_t = jnp.where(subl <= (lane & (HSUB - 1)), a_t, 0.0)
        a_full = jnp.where(same_block, jnp.concatenate([a_t[:, :CHUNK]] * n_sub, axis=0), 0.0)
        diags.append(_dot_tn(a_full, vs[h]))

    for h, (bi, hd) in enumerate(chains):
        b_last = bcums[h][CHUNK - 1:CHUNK, :]
        k_out = kks[h] * jnp.exp(b_last - bcums[h])
        state_scr[h] = states[h] * jnp.exp(b_last) + _dot_tn(vs[h], k_out)
        z = qfiz_ref[bi, :, col(3, hd)]
        o = inters[h] + offs[h] + diags[h]
        o_ref[bi, :, col(0, hd)] = _gated_head_norm(o, z, gn_ref[...]).astype(o_ref.dtype)


def _hgrn2(proj, lb, gn_b, layer, nb=2):
    batch, seq, _ = proj.shape
    hw = N_HEADS * HEAD_DIM
    return pl.pallas_call(
        _hgrn2_kernel,
        name="hgrn2",
        grid=(batch // nb, seq // CHUNK),
        in_specs=[
            pl.BlockSpec((nb, CHUNK, QKVZ_W), lambda b, n: (b, n, 1)),
            pl.BlockSpec((None, 1, hw), lambda b, n: (layer, 0, 0)),
            pl.BlockSpec((None, 1, HEAD_DIM), lambda b, n: (layer, 0, 0)),
        ],
        out_specs=pl.BlockSpec((nb, CHUNK, hw), lambda b, n: (b, n, 0)),
        out_shape=jax.ShapeDtypeStruct((batch, seq, hw), BF16),
        scratch_shapes=[
            pltpu.VMEM((nb * N_HEADS, HEAD_DIM, HEAD_DIM), F32),
            pltpu.VMEM((nb * N_HEADS, CHUNK, HEAD_DIM), F32),
            pltpu.VMEM((nb * N_HEADS, CHUNK, HEAD_DIM), F32),
        ],
        compiler_params=_params("parallel", "arbitrary"),
    )(proj, lb, gn_b)


def _memkv_kernel(m_ref, g_ref, w_ref, o_ref):
    h = _rms(m_ref[...], g_ref[...]).astype(BF16)
    o_ref[...] = jnp.dot(h, w_ref[...], preferred_element_type=F32).astype(o_ref.dtype)


def _memkv(mem2, g_mem, w_kv, layer, tn=512):
    m, d = mem2.shape
    n = w_kv.shape[-1]
    return pl.pallas_call(
        _memkv_kernel,
        name="memkv",
        grid=(n // tn,),
        in_specs=[
            pl.BlockSpec((m, d), lambda j: (0, 0)),
            pl.BlockSpec((None, 1, d), lambda j: (layer, 0, 0)),
            pl.BlockSpec((None, d, tn), lambda j: (layer, 0, j)),
        ],
        out_specs=pl.BlockSpec((m, tn), lambda j: (0, j)),
        out_shape=jax.ShapeDtypeStruct((m, n), BF16),
        compiler_params=_params("arbitrary"),
    )(mem2, g_mem, w_kv)


def _merge_kernel(oa_ref, ob_ref, q_ref, k_ref, v_ref, ga0, ga1, gb0, gb1, gc0, gc1,
                  wa_ref, wb_ref, wc_ref, o_ref):
    heads = []
    for h in range(MEM_HEADS):
        lo, hi = h * MEM_HEAD_DIM, (h + 1) * MEM_HEAD_DIM
        s = _dot_nt(q_ref[:, lo:hi], k_ref[:, lo:hi]) * (MEM_HEAD_DIM ** -0.5)
        p = jnp.exp(s - jnp.max(s, axis=-1, keepdims=True))
        p = p / jnp.sum(p, axis=-1, keepdims=True)
        heads.append(_dot(p, v_ref[:, lo:hi]))
    oc = jnp.concatenate(heads, axis=-1).astype(BF16)

    half = o_ref.shape[-1] // 2
    pa =jnp.dot(oa_ref[...], wa_ref[...], preferred_element_type=F32)
    pb = jnp.dot(ob_ref[...], wb_ref[...], preferred_element_type=F32)
    pc = jnp.dot(oc, wc_ref[...], preferred_element_type=F32)
    for part, (ra, rb, rc) in enumerate(((ga0, gb0, gc0), (ga1, gb1, gc1))):
        lo, hi = part * half, (part + 1) * half
        merged = (_sigmoid(ra[...]) * pa[:, lo:hi] + _sigmoid(rb[...]) * pb[:, lo:hi]
                  + _sigmoid(rc[...]) * pc[:, lo:hi])
        o_ref[:, lo:hi] = merged.astype(o_ref.dtype)


def _merge(oa, ob, proj, kv, w_a, w_b, w_c, layer, batch, n_mem, tm=256):
    t = oa.shape[0]
    d = w_a.shape[-1]
    cw = MEM_HEADS * MEM_HEAD_DIM
    half = d // 2
    tiles_per_batch = t // batch // tm
    q_blk = 2 * QKVZ_W // cw
    g_blk = (2 * QKVZ_W + cw) // half
    gate_specs = [pl.BlockSpec((tm, half), (lambda i, _j=j: (i, g_blk + _j))) for j in range(2 * N_BRANCH)]
    wspec = lambda: pl.BlockSpec((None, w_a.shape[1], d), lambda i: (layer, 0, 0))
    return pl.pallas_call(
        _merge_kernel,
        name="merge",
        grid=(t // tm,),
        in_specs=[
            pl.BlockSpec((tm, oa.shape[1]), lambda i: (i, 0)),
            pl.BlockSpec((tm, ob.shape[1]), lambda i: (i, 0)),
            pl.BlockSpec((tm, cw), lambda i: (i, q_blk)),
            pl.BlockSpec((n_mem, cw), lambda i: (i // tiles_per_batch, 0)),
            pl.BlockSpec((n_mem, cw), lambda i: (i // tiles_per_batch, 1)),
            *gate_specs,
            wspec(), wspec(), wspec(),
        ],
        out_specs=pl.BlockSpec((tm, d), lambda i: (i, 0)),
        out_shape=jax.ShapeDtypeStruct((t, d), BF16),
        compiler_params=_params("parallel"),
    )(oa, ob, proj, kv, kv, *([proj] * (2 * N_BRANCH)), w_a, w_b, w_c)


def _outproj_kernel(m_ref, w_ref, x_ref, g_ref, o_ref):
    y = jnp.dot(m_ref[...], w_ref[...], preferred_element_type=F32)
    o_ref[...] = x_ref[...] + _rms(y, g_ref[...])


def _outproj(merged, w_out, x2, g_post, layer, tm=512):
    t, d = x2.shape
    return pl.pallas_call(
        _outproj_kernel,
        name="outproj",
        grid=(t // tm,),
        in_specs=[
            pl.BlockSpec((tm, d), lambda i: (i, 0)),
            pl.BlockSpec((None, d, d), lambda i: (layer, 0, 0)),
            pl.BlockSpec((tm, d), lambda i: (i, 0)),
            pl.BlockSpec((None, 1, d), lambda i: (layer, 0, 0)),
        ],
        out_specs=pl.BlockSpec((tm, d), lambda i: (i, 0)),
        out_shape=jax.ShapeDtypeStruct((t, d), F32),
        compiler_params=_params("parallel"),
    )(merged, w_out, x2, g_post)


def _mlp_kernel(x_ref, gpre_ref, w1_ref, w2_ref, gpost_ref, gnext_ref, o_ref, hn_ref, h_scr):
    j = pl.program_id(1)

    @pl.when(j == 0)
    def _():
        h_scr[...] = _rms(x_ref[...], gpre_ref[...]).astype(BF16)
        o_ref[...] = jnp.zeros_like(o_ref)

    a = jnp.dot(h_scr[...], w1_ref[...], preferred_element_type=F32)
    a = jnp.maximum(a, 0.0)
    o_ref[...] += jnp.dot((a * a).astype(BF16), w2_ref[...], preferred_element_type=F32)

    @pl.when(j == pl.num_programs(1) - 1)
    def _():
        y = x_ref[...] + _rms(o_ref[...], gpost_ref[...])
        o_ref[...] = y
        hn_ref[...] = _rms(y, gnext_ref[...]).astype(hn_ref.dtype)


def _mlp(x2, g_pre, w1, w2, g_post, g_next, layer, next_layer, tm=1024, tf=512):
    t, d = x2.shape
    dff = w1.shape[-1]
    tm = min(tm, t)
    return pl.pallas_call(
        _mlp_kernel,
        name="mlp",
        grid=(t // tm, dff // tf),
        in_specs=[
            pl.BlockSpec((tm, d), lambda i, j: (i, 0), pipeline_mode=pl.Buffered(1)),
            pl.BlockSpec((None, 1, d), lambda i, j: (layer, 0, 0)),
            pl.BlockSpec((None, d, tf), lambda i, j: (layer, 0, j)),
            pl.BlockSpec((None, tf, d), lambda i, j: (layer, j, 0)),
            pl.BlockSpec((None, 1, d), lambda i, j: (layer, 0, 0)),
            pl.BlockSpec((None, 1, d), lambda i, j: (next_layer, 0, 0)),
        ],
        out_specs=[
            pl.BlockSpec((tm, d), lambda i, j: (i, 0), pipeline_mode=pl.Buffered(1)),
            pl.BlockSpec((tm, d), lambda i, j: (i, 0), pipeline_mode=pl.Buffered(1)),
        ],
        out_shape=[
            jax.ShapeDtypeStruct((t, d), F32),
            jax.ShapeDtypeStruct((t, d), BF16),
        ],
        scratch_shapes=[pltpu.VMEM((tm, d), BF16)],
        compiler_params=_params("parallel", "arbitrary"),
    )(x2, g_pre, w1, w2, g_post, g_next)


def kernel(x, mem, g_pre_mix, w_in, conv_a, a_log, dt_bias, gn_a, lb_raw, gn_b, g_mem, w_kv_mem, w_br_a, w_br_b, w_br_c, w_out, g_post_mix, g_pre_mlp, w_mlp_in, w_mlp_out, g_post_mlp):
    batch, seq, d = x.shape
    depth = w_in.shape[0]
    n_mem = mem.shape[1]
    hw = N_HEADS * HEAD_DIM
    x2 = x.reshape(batch * seq, d)
    mem2 = mem.reshape(batch * n_mem, d)

    small0 = QKVZ_W
    small1 = QKVZ_W + 2 * N_HEADS
    w_in = w_in.astype(BF16)
    w_main = jnp.concatenate([w_in[:, :, :small0], w_in[:, :, small1:]], axis=-1)
    w_small = jnp.pad(w_in[:, :, small0:small1], ((0, 0), (0, 0), (0, SMALL_W - 2 * N_HEADS)))
    pad_row = lambda p: jnp.pad(p, ((0, 0), (N_HEADS, SMALL_W - 2 * N_HEADS)))[:, None, :]
    alog_row = pad_row(a_log.astype(F32))
    dtb_row = pad_row(dt_bias.astype(F32))
    row = lambda p: p.astype(F32)[:, None, :]

    lb_all = jnp.cumsum(jax.nn.softmax(lb_raw.astype(F32), axis=0), axis=0)
    lb_all = (lb_all - lb_all[0])[:, None, :]

    w_kv = w_kv_mem.astype(BF16)
    w_a, w_b, w_c = w_br_a.astype(BF16), w_br_b.astype(BF16), w_br_c.astype(BF16)
    w_o = w_out.astype(BF16)
    w1, w2 = w_mlp_in.astype(BF16), w_mlp_out.astype(BF16)
    g_pre_mix, g_mem, g_post_mix, g_pre_mlp, g_post_mlp = map(row, (g_pre_mix, g_mem, g_post_mix, g_pre_mlp, g_post_mlp))
    gn_a, gn_b = row(gn_a), row(gn_b)
    conv_a = conv_a.astype(F32)

    h = _prenorm(x2, g_pre_mix, 0)
    for layer in range(depth):
        proj, small = _inproj(h, w_main, w_small, layer)
        proj3 = proj.reshape(batch, seq, -1)
        oa = _deltanet(proj3, small.reshape(batch, seq, -1), conv_a, alog_row, dtb_row, gn_a, layer)
        ob = _hgrn2(proj3, lb_all, gn_b, layer)
        oa, ob = oa.reshape(batch * seq, hw), ob.reshape(batch * seq, hw)
        kv = _memkv(mem2, g_mem, w_kv, layer)
        merged = _merge(oa, ob, proj, kv, w_a, w_b, w_c, layer, batch, n_mem)
        x2 = _outproj(merged, w_o, x2, g_post_mix, layer)
        x2, h = _mlp(x2, g_pre_mlp, w1, w2, g_post_mlp, g_pre_mix, layer, (layer + 1) % depth)
    return x2.reshape(batch, seq, d)
```

```python
import functools

import jax
import jax.numpy as jnp
from jax import lax
from jax.experimental import pallas as pl
from jax.experimental.pallas import tpu as pltpu

F32 = jnp.float32
BF16 = jnp.bfloat16

EPS = 1e-6
CHUNK = 64
SUB = 4
HSUB = 8
HEAD_DIM = 128
N_HEADS = 8
CONV_TAPS = 4
MEM_HEADS = 4
MEM_HEAD_DIM = 256
N_BRANCH = 3
QKVZ_W = 4 * N_HEADS * HEAD_DIM
SMALL_W = 128
VMEM_LIMIT = 56 * 1024 * 1024


def _params(*sem):
    return pltpu.CompilerParams(dimension_semantics=sem, vmem_limit_bytes=VMEM_LIMIT)


def _sigmoid(x):
    return 1.0 / (1.0 + jnp.exp(-x))


def _silu(x):
    return x * _sigmoid(x)


def _rms(x, g):
    return x * lax.rsqrt(jnp.mean(x * x, axis=-1, keepdims=True) + EPS) * g


def _dot(a, b):
    return jnp.dot(a.astype(BF16), b.astype(BF16), preferred_element_type=F32)


def _dot_nt(a, b):
    return lax.dot_general(a.astype(BF16), b.astype(BF16), (((1,), (1,)), ((), ())),
                           preferred_element_type=F32)


def _dot_tn(a, b):
    return lax.dot_general(a.astype(BF16), b.astype(BF16), (((0,), (0,)), ((), ())),
                           preferred_element_type=F32)


def _split3(x):
    hi = x.astype(BF16)
    r1 = x - hi.astype(F32)
    mid = r1.astype(BF16)
    lo = (r1 - mid.astype(F32)).astype(BF16)
    return hi, mid, lo


def _dot_sel(sel, x):
    hi, mid, lo = _split3(x)
    d = lambda t: jnp.dot(sel, t, preferred_element_type=F32)
    return d(hi) + d(mid) + d(lo)


def _tril_ones(n):
    r = lax.broadcasted_iota(jnp.int32, (n, n), 0)
    c = lax.broadcasted_iota(jnp.int32, (n, n), 1)
    return (r >= c).astype(BF16)


def _prenorm_kernel(x_ref, g_ref, o_ref):
    o_ref[...] = _rms(x_ref[...], g_ref[...]).astype(o_ref.dtype)


def _prenorm(x2, g, layer, tm=1024):
    t, d = x2.shape
    tm = min(tm, t)
    return pl.pallas_call(
        _prenorm_kernel,
        name="prenorm",
        grid=(t // tm,),
        in_specs=[
            pl.BlockSpec((tm, d), lambda i: (i, 0)),
            pl.BlockSpec((None, 1, d), lambda i: (layer, 0, 0)),
        ],
        out_specs=pl.BlockSpec((tm, d), lambda i: (i, 0)),
        out_shape=jax.ShapeDtypeStruct((t, d), BF16),
        compiler_params=_params("parallel"),
    )(x2, g)


def _inproj_kernel(tiles_a, h_ref, wa_ref, wr_ref, ws_ref, o_ref, os_ref):
    j = pl.program_id(1)

    @pl.when(j == 0)
    def _():
        os_ref[...] = jnp.dot(h_ref[...], ws_ref[...], preferred_element_type=F32)

    @pl.when(j < tiles_a)
    def _():
        o_ref[...] = jnp.dot(h_ref[...], wa_ref[...], preferred_element_type=F32)

    @pl.when(j >= tiles_a)
    def _():
        o_ref[...] = jnp.dot(h_ref[...], wr_ref[...], preferred_element_type=F32)


def _inproj(h, w_a, w_rest, w_small, layer, tm=2048, tn=512):
    t, d = h.shape
    tiles_a = w_a.shape[-1] // tn
    tiles_rest = w_rest.shape[-1] // tn
    n = (tiles_a + tiles_rest) * tn
    tm = min(tm, t)
    return pl.pallas_call(
        functools.partial(_inproj_kernel, tiles_a),
        name="inproj",
        grid=(t // tm, tiles_a + tiles_rest),
        in_specs=[
            pl.BlockSpec((tm, d), lambda i, j: (i, 0)),
            pl.BlockSpec((None, d, tn), lambda i, j: (layer, 0, jnp.minimum(j, tiles_a - 1))),
            pl.BlockSpec((None, d, tn), lambda i, j: (layer, 0, jnp.maximum(j - tiles_a, 0))),
            pl.BlockSpec((None, d, SMALL_W), lambda i, j: (layer, 0, 0)),
        ],
        out_specs=[
            pl.BlockSpec((tm, tn), lambda i, j: (i, j)),
            pl.BlockSpec((tm, SMALL_W), lambda i, j: (i, 0)),
        ],
        out_shape=[
            jax.ShapeDtypeStruct((t, n), F32),
            jax.ShapeDtypeStruct((t, SMALL_W), F32),
        ],
        compiler_params=_params("parallel", "arbitrary"),
    )(h, w_a, w_rest, w_small)


def _gated_head_norm(o, z, g):
    return o * lax.rsqrt(jnp.mean(o * o, axis=-1, keepdims=True) + EPS) * g * _silu(z)


def _unit_lower_inverses(lows, dot):
    n = lows[0].shape[0]
    r = lax.broadcasted_iota(jnp.int32, (n, n), 0)
    c = lax.broadcasted_iota(jnp.int32, (n, n), 1)
    eye = (r == c).astype(F32)
    same = lambda s: (r >> (s.bit_length() - 1)) == (c >> (s.bit_length() - 1))
    size = SUB
    pows = [jnp.where(same(size), low, 0.0) for low in lows]
    invs = [eye - b for b in pows]
    order = 2
    while order < SUB:
        pows = [dot(p, p) for p in pows]
        invs = [i + dot(i, p) for i, p in zip(invs, pows)]
        order *= 2
        yield
    while size < n:
        offs = [jnp.where(same(2 * size), jnp.where(same(size), 0.0, low), 0.0) for low in lows]
        tmp = [dot(i, o) for i, o in zip(invs, offs)]
        yield
        invs = [i - dot(t, i) for i, t in zip(invs, tmp)]
        size *= 2
        yield
    return invs


def _deltanet_body(first, qkvz_ref, small_ref, conv_ref, alog_ref, dtb_ref, gn_ref, o_ref,
                   state_scr, ext_scr):
    hw = N_HEADS * HEAD_DIM
    nb = qkvz_ref.shape[0]
    chains = [(bi, h) for bi in range(nb) for h in range(N_HEADS)]

    @pl.when(first)
    def _():
        state_scr[...] = jnp.zeros_like(state_scr)
        ext_scr[:, 0:8, :] = jnp.zeros((nb, 8, 3 * hw), F32)

    for bi in range(nb):
        ext_scr[bi, 8:8 + CHUNK, :] = qkvz_ref[bi, :, 0:3 * hw]

    r = lax.broadcasted_iota(jnp.int32, (CHUNK, CHUNK), 0)
    c = lax.broadcasted_iota(jnp.int32, (CHUNK, CHUNK), 1)
    causal = r >= c
    strict = r > c
    tril = _tril_ones(CHUNK)

    beta_all, gam_all, gam_t = [], [], []
    for bi in range(nb):
        sm = small_ref[bi]
        beta_all.append(_sigmoid(sm))
        pre = sm + dtb_ref[...]
        softplus = jnp.maximum(pre, 0.0) + jnp.log(1.0 + jnp.exp(-jnp.abs(pre)))
        logdecay = -jnp.exp(alog_ref[...]) * softplus
        gam = _dot_sel(tril, logdecay)
        gam_all.append(gam)
        gam_t.append(jnp.concatenate([gam, jnp.zeros_like(gam)], axis=0).T)

    ext_rows = ext_scr.shape[1]
    sel_d = (lax.broadcasted_iota(jnp.int32, (CHUNK, CONV_TAPS * ext_rows), 1)
             - lax.broadcasted_iota(jnp.int32, (CHUNK, CONV_TAPS * ext_rows), 0) - (8 - (CONV_TAPS - 1)))
    sel = sel_d == 0
    for k in range(1, CONV_TAPS):
        sel = sel | (sel_d == k * (ext_rows + 1))
    sel = sel.astype(BF16)

    def conv(sec, bi):
        staged = ext_scr[bi, :, sec * hw:(sec + 1) * hw]
        taps = [staged * conv_ref[k:k + 1, sec * hw:(sec + 1) * hw] for k in range(CONV_TAPS)]
        out = _silu(_dot(sel, jnp.concatenate(taps, axis=0)))
        return [out[:, h * HEAD_DIM:(h + 1) * HEAD_DIM] for h in range(N_HEADS)]

    qs = [q for bi in range(nb) for q in conv(0, bi)]
    yield
    ks = [k for bi in range(nb) for k in conv(1, bi)]
    yield
    vs = [v for bi in range(nb) for v in conv(2, bi)]
    yield
    qs = [q * lax.rsqrt(jnp.sum(q * q, axis=-1, keepdims=True) + EPS) * (HEAD_DIM ** -0.5) for q in qs]
    ks = [k * lax.rsqrt(jnp.sum(k * k, axis=-1, keepdims=True) + EPS) for k in ks]

    yield
    betas = [beta_all[bi][:, h:h + 1] for bi, h in chains]
    gcols = [gam_all[bi][:, N_HEADS + h:N_HEADS + h + 1] for bi, h in chains]
    grows = [gam_t[bi][N_HEADS + h:N_HEADS + h + 1, 0:CHUNK] for bi, h in chains]
    egs = [jnp.exp(g) for g in gcols]
    decs = [jnp.exp(jnp.where(causal, gc - gr, -jnp.inf)) for gc, gr in zip(gcols, grows)]
    yield
    kbs = [k * b for k, b in zip(ks, betas)]
    lows = [jnp.where(strict, _dot_nt(kb, k) * d, 0.0) for kb, k, d in zip(kbs, ks, decs)]
    a_qks = [_dot_nt(q, k) * d for q, k, d in zip(qs, ks, decs)]
    yield
    invs = yield from _unit_lower_inverses(lows, _dot)
    yield
    sols = [_dot(i, jnp.concatenate([v * b, kb * eg], axis=-1))
            for i, v, b, kb, eg in zip(invs, vs, betas, kbs, egs)]

    yield
    states = [state_scr[n] for n in range(len(chains))]
    u_news = [sol[:, :HEAD_DIM] - _dot(sol[:, HEAD_DIM:], st) for sol, st in zip(sols, states)]
    yield
    outs = [_dot(q * eg, st) + _dot(a, un) for q, eg, st, a, un in zip(qs, egs, states, a_qks, u_news)]
    yield
    for n in range(len(chains)):
        k_dec = ks[n] * jnp.exp(gcols[n][CHUNK - 1:CHUNK, :] - gcols[n])
        state_scr[n] = egs[n][CHUNK - 1:CHUNK, :] * states[n] + _dot_tn(k_dec, u_news[n])
    yield
    for n, (bi, h) in enumerate(chains):
        lo = h * HEAD_DIM
        z = qkvz_ref[bi, :, 3 * hw + lo:3 * hw + lo + HEAD_DIM]
        o_ref[bi, :, lo:lo + HEAD_DIM] = _gated_head_norm(outs[n], z, gn_ref[...]).astype(o_ref.dtype)

    for bi in range(nb):
        ext_scr[bi, 0:8, :] = ext_scr[bi, CHUNK:CHUNK + 8, :]


def _hgrn2_body(first, qfiz_ref, lb_ref, gn_ref, o_ref, state_scr, q_scr, b_scr):
    hw = N_HEADS * HEAD_DIM
    nb = qfiz_ref.shape[0]
    chains = [(bi, h) for bi in range(nb) for h in range(N_HEADS)]
    heads = range(len(chains))
    col = lambda sec, h: slice(sec * hw + h * HEAD_DIM, sec * hw + (h + 1) * HEAD_DIM)
    n_sub = CHUNK // HSUB

    @pl.when(first)
    def _():
        state_scr[...] = jnp.zeros_like(state_scr)

    tril = _tril_ones(CHUNK)
    lane = lax.broadcasted_iota(jnp.int32, (HSUB, HEAD_DIM), 1)
    subl = lax.broadcasted_iota(jnp.int32, (HSUB, HEAD_DIM), 0)
    r = lax.broadcasted_iota(jnp.int32, (CHUNK, CHUNK), 0)
    c = lax.broadcasted_iota(jnp.int32, (CHUNK, CHUNK), 1)
    same_block = (r >> (HSUB.bit_length() - 1)) == (c >> (HSUB.bit_length() - 1))

    lbs = [lb_ref[:, col(0, h)] for _, h in chains]
    fs = [qfiz_ref[bi, :, col(1, h)] for bi, h in chains]
    vs = [qfiz_ref[bi, :, col(2, h)] for bi, h in chains]
    qs = [_silu(qfiz_ref[bi, :, col(0, h)]) * (HEAD_DIM ** -0.5) for bi, h in chains]
    yield
    log_sigs = [jnp.minimum(f, 0.0) - jnp.log(1.0 + jnp.exp(-jnp.abs(f))) for f in fs]
    las = [jnp.log(lb) for lb in lbs]
    l1s = [jnp.log(1.0 - lb) for lb in lbs]
    lcs = [l1 + ls for l1, ls in zip(l1s, log_sigs)]
    yield
    log_fs = [jnp.maximum(la, lc) + jnp.log(1.0 + jnp.exp(-jnp.abs(la - lc))) for la, lc in zip(las, lcs)]
    log_ks = [lc - f for lc, f in zip(lcs, fs)]
    kks = [jnp.exp(lk) for lk in log_ks]
    yield
    bcums = [_dot_sel(tril, lf) for lf in log_fs]
    bcs = [b - lk for b, lk in zip(bcums, log_ks)]
    yield
    states = [state_scr[h] for h in heads]
    inters = [_dot_nt(q * jnp.exp(b), st) for q, b, st in zip(qs, bcums, states)]
    for h in heads:
        q_scr[h] = qs[h]
        b_scr[h] = bcums[h]

    yield
    a_offs = []
    for h in heads:
        blocks = [jnp.zeros((HSUB, CHUNK), F32)]
        for i in range(1, n_sub):
            r0 = i * HSUB
            ref_row = bcums[h][r0:r0 + 1, :]
            q_in = qs[h][r0:r0 + HSUB, :] * jnp.exp(bcums[h][r0:r0 + HSUB, :] - ref_row)
            k_in = kks[h][0:r0, :] * jnp.exp(ref_row - bcums[h][0:r0, :])
            p = _dot_nt(q_in, k_in)
            blocks.append(jnp.concatenate([p, jnp.zeros((HSUB, CHUNK - r0), F32)], axis=1))
        a_offs.append(jnp.concatenate(blocks, axis=0))
        if h % 4 == 3:
            yield
    yield
    offs = [_dot(a, v) for a, v in zip(a_offs, vs)]

    yield
    diags = []
    for h in heads:
        a_t = jnp.zeros((HSUB, HEAD_DIM), F32)
        for i in range(n_sub):
            r0 = i * HSUB
            bc_i = bcs[h][r0:r0 + HSUB, :]
            for t in range(r0, r0 + HSUB):
                x = q_scr[h, t:t + 1, :] * jnp.exp(jnp.minimum(b_scr[h, t:t + 1, :] - bc_i, 0.0))
                a_t = jnp.where(lane == t, jnp.sum(x, axis=-1, keepdims=True), a_t)
        a_t = jnp.where(subl <= (lane & (HSUB - 1)), a_t, 0.0)
        a_full = jnp.where(same_block, jnp.concatenate([a_t[:, :CHUNK]] * n_sub, axis=0), 0.0)
        diags.append(_dot_tn(a_full, vs[h]))
        if h % 2:
            yield

    yield
    for h, (bi, hd) in enumerate(chains):
        b_last = bcums[h][CHUNK - 1:CHUNK, :]
        k_out = kks[h] * jnp.exp(b_last - bcums[h])
        state_scr[h] = states[h] * jnp.exp(b_last) + _dot_tn(vs[h], k_out)
        z = qfiz_ref[bi, :, col(3, hd)]
        o = inters[h] + offs[h] + diags[h]
        o_ref[bi, :, col(0, hd)] = _gated_head_norm(o, z, gn_ref[...]).astype(o_ref.dtype)


def _interleave(*stage_generators):
    pending = list(stage_generators)
    while pending:
        for gen in list(pending):
            try:
                next(gen)
            except StopIteration:
                pending.remove(gen)


def _rec_kernel(qkvz_ref, qfiz_ref, small_ref, conv_ref, alog_ref, dtb_ref, gna_ref, lb_ref, gnb_ref,
                oa_ref, ob_ref, dn_state, ext_scr, hg_state, q_scr, b_scr):
    first = pl.program_id(1) == 0
    _interleave(
        _deltanet_body(first, qkvz_ref, small_ref, conv_ref, alog_ref, dtb_ref, gna_ref, oa_ref, dn_state, ext_scr),
        _hgrn2_body(first, qfiz_ref, lb_ref, gnb_ref, ob_ref, hg_state, q_scr, b_scr))


def _rec(proj, small, conv_a, alog_row, dtb_row, gn_a, lb, gn_b, layer, nb=2):
    batch, seq, _ = proj.shape
    hw = N_HEADS * HEAD_DIM
    row = lambda width: pl.BlockSpec((None, 1, width), lambda b, n: (layer, 0, 0))
    return pl.pallas_call(
        _rec_kernel,
        name="rec",
        grid=(batch // nb, seq // CHUNK),
        in_specs=[
            pl.BlockSpec((nb, CHUNK, QKVZ_W), lambda b, n: (b, n, 0)),
            pl.BlockSpec((nb, CHUNK, QKVZ_W), lambda b, n: (b, n, 1)),
            pl.BlockSpec((nb, CHUNK, SMALL_W), lambda b, n: (b, n, 0)),
            pl.BlockSpec((None, CONV_TAPS, 3 * hw), lambda b, n: (layer, 0, 0)),
            row(SMALL_W), row(SMALL_W), row(HEAD_DIM), row(hw), row(HEAD_DIM),
        ],
        out_specs=[pl.BlockSpec((nb, CHUNK, hw), lambda b, n: (b, n, 0))] * 2,
        out_shape=[jax.ShapeDtypeStruct((batch, seq, hw), BF16)] * 2,
        scratch_shapes=[
            pltpu.VMEM((nb * N_HEADS, HEAD_DIM, HEAD_DIM), F32),
            pltpu.VMEM((nb, CHUNK + 8, 3 * hw), F32),
            pltpu.VMEM((nb * N_HEADS, HEAD_DIM, HEAD_DIM), F32),
            pltpu.VMEM((nb * N_HEADS, CHUNK, HEAD_DIM), F32),
            pltpu.VMEM((nb * N_HEADS, CHUNK, HEAD_DIM), F32),
        ],
        compiler_params=_params("parallel", "arbitrary"),
    )(proj, proj, small, conv_a, alog_row, dtb_row, gn_a, lb, gn_b)


def _memkv_kernel(m_ref, g_ref, w_ref, o_ref):
    h = _rms(m_ref[...], g_ref[...]).astype(BF16)
    o_ref[...] = jnp.dot(h, w_ref[...], preferred_element_type=F32).astype(o_ref.dtype)


def _memkv(mem2, g_mem, w_kv, layer, tn=512):
    m, d = mem2.shape
    n = w_kv.shape[-1]
    return pl.pallas_call(
        _memkv_kernel,
        name="memkv",
        grid=(n // tn,),
        in_specs=[
            pl.BlockSpec((m, d), lambda j: (0, 0)),
            pl.BlockSpec((None, 1, d), lambda j: (layer, 0, 0)),
            pl.BlockSpec((None, d, tn), lambda j: (layer, 0, j)),
        ],
        out_specs=pl.BlockSpec((m, tn), lambda j: (0, j)),
        out_shape=jax.ShapeDtypeStruct((m, n), BF16),
        compiler_params=_params("arbitrary"),
    )(mem2, g_mem, w_kv)


def _merge_kernel(oa_ref, ob_ref, q_ref, k_ref, v_ref, ga0, ga1, gb0, gb1, gc0, gc1,
                  wa_ref, wb_ref, wc_ref, o_ref):
    heads = []
    for h in range(MEM_HEADS):
        lo, hi = h * MEM_HEAD_DIM, (h + 1) * MEM_HEAD_DIM
        s = _dot_nt(q_ref[:, lo:hi], k_ref[:, lo:hi]) * (MEM_HEAD_DIM ** -0.5)
        p = jnp.exp(s - jnp.max(s, axis=-1, keepdims=True))
        p = p / jnp.sum(p, axis=-1, keepdims=True)
        heads.append(_dot(p, v_ref[:, lo:hi]))
    oc = jnp.concatenate(heads, axis=-1).astype(BF16)

    half = o_ref.shape[-1] // 2
    pa =jnp.dot(oa_ref[...], wa_ref[...], preferred_element_type=F32)
    pb = jnp.dot(ob_ref[...], wb_ref[...], preferred_element_type=F32)
    pc = jnp.dot(oc, wc_ref[...], preferred_element_type=F32)
    for part, (ra, rb, rc) in enumerate(((ga0, gb0, gc0), (ga1, gb1, gc1))):
        lo, hi = part * half, (part + 1) * half
        merged = (_sigmoid(ra[...]) * pa[:, lo:hi] + _sigmoid(rb[...]) * pb[:, lo:hi]
                  + _sigmoid(rc[...]) * pc[:, lo:hi])
        o_ref[:, lo:hi] = merged.astype(o_ref.dtype)


def _merge(oa, ob, proj, kv, w_a, w_b, w_c, layer, batch, n_mem, tm=256):
    t = oa.shape[0]
    d = w_a.shape[-1]
    cw = MEM_HEADS * MEM_HEAD_DIM
    half = d // 2
    tiles_per_batch = t // batch // tm
    q_blk = 2 * QKVZ_W // cw
    g_blk = (2 * QKVZ_W + cw) // half
    gate_specs = [pl.BlockSpec((tm, half), (lambda i, _j=j: (i, g_blk + _j))) for j in range(2 * N_BRANCH)]
    wspec = lambda: pl.BlockSpec((None, w_a.shape[1], d), lambda i: (layer, 0, 0))
    return pl.pallas_call(
        _merge_kernel,
        name="merge",
        grid=(t // tm,),
        in_specs=[
            pl.BlockSpec((tm, oa.shape[1]), lambda i: (i, 0)),
            pl.BlockSpec((tm, ob.shape[1]), lambda i: (i, 0)),
            pl.BlockSpec((tm, cw), lambda i: (i, q_blk)),
            pl.BlockSpec((n_mem, cw), lambda i: (i // tiles_per_batch, 0)),
            pl.BlockSpec((n_mem, cw), lambda i: (i // tiles_per_batch, 1)),
            *gate_specs,
            wspec(), wspec(), wspec(),
        ],
        out_specs=pl.BlockSpec((tm, d), lambda i: (i, 0)),
        out_shape=jax.ShapeDtypeStruct((t, d), BF16),
        compiler_params=_params("parallel"),
    )(oa, ob, proj, kv, kv, *([proj] * (2 * N_BRANCH)), w_a, w_b, w_c)


def _outproj_kernel(m_ref, w_ref, x_ref, g_ref, o_ref):
    y = jnp.dot(m_ref[...], w_ref[...], preferred_element_type=F32)
    o_ref[...] = x_ref[...] + _rms(y, g_ref[...])


def _outproj(merged, w_out, x2, g_post, layer, tm=512):
    t, d = x2.shape
    return pl.pallas_call(
        _outproj_kernel,
        name="outproj",
        grid=(t // tm,),
        in_specs=[
            pl.BlockSpec((tm, d), lambda i: (i, 0)),
            pl.BlockSpec((None, d, d), lambda i: (layer, 0, 0)),
            pl.BlockSpec((tm, d), lambda i: (i, 0)),
            pl.BlockSpec((None, 1, d), lambda i: (layer, 0, 0)),
        ],
        out_specs=pl.BlockSpec((tm, d), lambda i: (i, 0)),
        out_shape=jax.ShapeDtypeStruct((t, d), F32),
        compiler_params=_params("parallel"),
    )(merged, w_out, x2, g_post)


def _mlp_kernel(x_ref, gpre_ref, w1_ref, w2_ref, gpost_ref, gnext_ref, o_ref, hn_ref, h_scr):
    j = pl.program_id(1)

    @pl.when(j == 0)
    def _():
        h_scr[...] = _rms(x_ref[...], gpre_ref[...]).astype(BF16)
        o_ref[...] = jnp.zeros_like(o_ref)

    a = jnp.dot(h_scr[...], w1_ref[...], preferred_element_type=F32)
    a = jnp.maximum(a, 0.0)
    o_ref[...] += jnp.dot((a * a).astype(BF16), w2_ref[...], preferred_element_type=F32)

    @pl.when(j == pl.num_programs(1) - 1)
    def _():
        y = x_ref[...] + _rms(o_ref[...], gpost_ref[...])
        o_ref[...] = y
        hn_ref[...] = _rms(y, gnext_ref[...]).astype(hn_ref.dtype)


def _mlp(x2, g_pre, w1, w2, g_post, g_next, layer, next_layer, tm=512, tf=512):
    t, d = x2.shape
    dff = w1.shape[-1]
    tm = min(tm, t)
    return pl.pallas_call(
        _mlp_kernel,
        name="mlp",
        grid=(t // tm, dff // tf),
        in_specs=[
            pl.BlockSpec((tm, d), lambda i, j: (i, 0)),
            pl.BlockSpec((None, 1, d), lambda i, j: (layer, 0, 0)),
            pl.BlockSpec((None, d, tf), lambda i, j: (layer, 0, j)),
            pl.BlockSpec((None, tf, d), lambda i, j: (layer, j, 0)),
            pl.BlockSpec((None, 1, d), lambda i, j: (layer, 0, 0)),
            pl.BlockSpec((None, 1, d), lambda i, j: (next_layer, 0, 0)),
        ],
        out_specs=[
            pl.BlockSpec((tm, d), lambda i, j: (i, 0)),
            pl.BlockSpec((tm, d), lambda i, j: (i, 0)),
        ],
        out_shape=[
            jax.ShapeDtypeStruct((t, d), F32),
            jax.ShapeDtypeStruct((t, d), BF16),
        ],
        scratch_shapes=[pltpu.VMEM((tm, d), BF16)],
        compiler_params=_params("parallel", "arbitrary"),
    )(x2, g_pre, w1, w2, g_post, g_next)


def kernel(x, mem, g_pre_mix, w_in, conv_a, a_log, dt_bias, gn_a, lb_raw, gn_b, g_mem, w_kv_mem, w_br_a, w_br_b, w_br_c, w_out, g_post_mix, g_pre_mlp, w_mlp_in, w_mlp_out, g_post_mlp):
    batch, seq, d = x.shape
    depth = w_in.shape[0]
    n_mem = mem.shape[1]
    hw = N_HEADS * HEAD_DIM
    x2 = x.reshape(batch * seq, d)
    mem2 = mem.reshape(batch * n_mem, d)

    small0 = QKVZ_W
    small1 = QKVZ_W + 2 * N_HEADS
    w_in_a = w_in[:, :, :small0].astype(BF16)
    w_in_rest = w_in[:, :, small1:].astype(BF16)
    w_small = jnp.pad(w_in[:, :, small0:small1], ((0, 0), (0, 0), (0, SMALL_W - 2 * N_HEADS))).astype(BF16)
    pad_row = lambda p: jnp.pad(p, ((0, 0), (N_HEADS, SMALL_W - 2 * N_HEADS)))[:, None, :]
    alog_row = pad_row(a_log.astype(F32))
    dtb_row = pad_row(dt_bias.astype(F32))
    row = lambda p: p.astype(F32)[:, None, :]

    lb_all = jnp.cumsum(jax.nn.softmax(lb_raw.astype(F32), axis=0), axis=0)
    lb_all = (lb_all - lb_all[0])[:, None, :]

    w_kv = w_kv_mem.astype(BF16)
    w_a, w_b, w_c = w_br_a.astype(BF16), w_br_b.astype(BF16), w_br_c.astype(BF16)
    w_o = w_out.astype(BF16)
    w1, w2 = w_mlp_in.astype(BF16), w_mlp_out.astype(BF16)
    g_pre_mix, g_mem, g_post_mix, g_pre_mlp, g_post_mlp = map(row, (g_pre_mix, g_mem, g_post_mix, g_pre_mlp, g_post_mlp))
    gn_a, gn_b = row(gn_a), row(gn_b)
    conv_a = conv_a.astype(F32)

    h = _prenorm(x2, g_pre_mix, 0)
    for layer in range(depth):
        proj, small = _inproj(h, w_in_a, w_in_rest, w_small, layer)
        oa, ob = _rec(proj.reshape(batch, seq, -1), small.reshape(batch, seq, -1), conv_a, alog_row, dtb_row, gn_a,
                      lb_all, gn_b, layer)
        oa, ob = oa.reshape(batch * seq, hw), ob.reshape(batch * seq, hw)
        kv = _memkv(mem2, g_mem, w_kv, layer)
        merged = _merge(oa, ob, proj, kv, w_a, w_b, w_c, layer, batch, n_mem)
        x2 = _outproj(merged, w_o, x2, g_post_mix, layer)
        x2, h = _mlp(x2, g_pre_mlp, w1, w2, g_post_mlp, g_pre_mix, layer, (layer + 1) % depth)
    return x2.reshape(batch, seq, d)
```

```python
import functools

import jax
import jax.numpy as jnp
from jax import lax
from jax.experimental import pallas as pl
from jax.experimental.pallas import tpu as pltpu

F32 = jnp.float32
BF16 = jnp.bfloat16

EPS = 1e-6
CHUNK = 64
SUB = 4
HSUB = 8
HEAD_DIM = 128
N_HEADS = 8
CONV_TAPS = 4
MEM_HEADS = 4
MEM_HEAD_DIM = 256
N_BRANCH = 3
QKVZ_W = 4 * N_HEADS * HEAD_DIM
SMALL_W = 128
VMEM_LIMIT = 56 * 1024 * 1024


def _params(*sem):
    return pltpu.CompilerParams(dimension_semantics=sem, vmem_limit_bytes=VMEM_LIMIT)


def _sigmoid(x):
    return 1.0 / (1.0 + jnp.exp(-x))


def _silu(x):
    return x * _sigmoid(x)


def _rms(x, g):
    return x * lax.rsqrt(jnp.mean(x * x, axis=-1, keepdims=True) + EPS) * g


def _dot(a, b):
    return jnp.dot(a.astype(BF16), b.astype(BF16), preferred_element_type=F32)


def _dot_nt(a, b):
    return lax.dot_general(a.astype(BF16), b.astype(BF16), (((1,), (1,)), ((), ())),
                           preferred_element_type=F32)


def _dot_tn(a, b):
    return lax.dot_general(a.astype(BF16), b.astype(BF16), (((0,), (0,)), ((), ())),
                           preferred_element_type=F32)


def _split3(x):
    hi = x.astype(BF16)
    r1 = x - hi.astype(F32)
    mid = r1.astype(BF16)
    lo = (r1 - mid.astype(F32)).astype(BF16)
    return hi, mid, lo


def _dot_sel(sel, x):
    hi, mid, lo = _split3(x)
    d = lambda t: jnp.dot(sel, t, preferred_element_type=F32)
    return d(hi) + d(mid) + d(lo)


def _tril_ones(n):
    r = lax.broadcasted_iota(jnp.int32, (n, n), 0)
    c = lax.broadcasted_iota(jnp.int32, (n, n), 1)
    return (r >= c).astype(BF16)


def _prenorm_kernel(x_ref, g_ref, o_ref):
    o_ref[...] = _rms(x_ref[...], g_ref[...]).astype(o_ref.dtype)


def _prenorm(x2, g, layer, tm=1024):
    t, d = x2.shape
    tm = min(tm, t)
    return pl.pallas_call(
        _prenorm_kernel,
        name="prenorm",
        grid=(t // tm,),
        in_specs=[
            pl.BlockSpec((tm, d), lambda i: (i, 0)),
            pl.BlockSpec((None, 1, d), lambda i: (layer, 0, 0)),
        ],
        out_specs=pl.BlockSpec((tm, d), lambda i: (i, 0)),
        out_shape=jax.ShapeDtypeStruct((t, d), BF16),
        compiler_params=_params("parallel"),
    )(x2, g)


def _repack_kernel(shift, a_ref, b_ref, o_ref):
    o_ref[...] = jnp.concatenate([a_ref[:, shift:], b_ref[:, :shift]], axis=1).astype(o_ref.dtype)


def _repack_tail(w, start, td=2048, tn=512):
    depth, d, cols = w.shape
    td = min(td, d)
    lane = 128
    shift = start % lane
    first = (start - shift) // tn
    n = cols - start
    return pl.pallas_call(
        functools.partial(_repack_kernel, shift),
        name="repack",
        grid=(depth, d // td, n // tn),
        in_specs=[
            pl.BlockSpec((None, td, tn), lambda l, i, j: (l, i, first + j)),
            pl.BlockSpec((None, td, lane), lambda l, i, j: (l, i, (first + j + 1) * (tn // lane))),
        ],
        out_specs=pl.BlockSpec((None, td, tn), lambda l, i, j: (l, i, j)),
        out_shape=jax.ShapeDtypeStruct((depth, d, n), BF16),
        compiler_params=_params("parallel", "parallel", "parallel"),
    )(w, w)


def _inproj_kernel(tiles_a, h_ref, wa_ref, wr_ref, ws_ref, o_ref, os_ref):
    j = pl.program_id(1)

    @pl.when(j == 0)
    def _():
        os_ref[...] = jnp.dot(h_ref[...], ws_ref[...], preferred_element_type=F32)

    @pl.when(j < tiles_a)
    def _():
        o_ref[...] = jnp.dot(h_ref[...], wa_ref[...], preferred_element_type=F32)

    @pl.when(j >= tiles_a)
    def _():
        o_ref[...] = jnp.dot(h_ref[...], wr_ref[...], preferred_element_type=F32)


def _inproj(h, w_a, w_rest, w_small, layer, tm=2048, tn=512):
    t, d = h.shape
    tiles_a = w_a.shape[-1] // tn
    tiles_rest = w_rest.shape[-1] // tn
    n = (tiles_a + tiles_rest) * tn
    tm = min(tm, t)
    return pl.pallas_call(
        functools.partial(_inproj_kernel, tiles_a),
        name="inproj",
        grid=(t // tm, tiles_a + tiles_rest),
        in_specs=[
            pl.BlockSpec((tm, d), lambda i, j: (i, 0)),
            pl.BlockSpec((None, d, tn), lambda i, j: (layer, 0, jnp.minimum(j, tiles_a - 1))),
            pl.BlockSpec((None, d, tn), lambda i, j: (layer, 0, jnp.maximum(j - tiles_a, 0))),
            pl.BlockSpec((None, d, SMALL_W), lambda i, j: (layer, 0, 0)),
        ],
        out_specs=[
            pl.BlockSpec((tm, tn), lambda i, j: (i, j)),
            pl.BlockSpec((tm, SMALL_W), lambda i, j: (i, 0)),
        ],
        out_shape=[
            jax.ShapeDtypeStruct((t, n), F32),
            jax.ShapeDtypeStruct((t, SMALL_W), F32),
        ],
        compiler_params=_params("parallel", "arbitrary"),
    )(h, w_a, w_rest, w_small)


def _gated_head_norm(o, z, g):
    return o * lax.rsqrt(jnp.mean(o * o, axis=-1, keepdims=True) + EPS) * g * _silu(z)


def _unit_lower_inverses(lows, dot):
    n = lows[0].shape[0]
    r = lax.broadcasted_iota(jnp.int32, (n, n), 0)
    c = lax.broadcasted_iota(jnp.int32, (n, n), 1)
    eye = (r == c).astype(F32)
    same = lambda s: (r >> (s.bit_length() - 1)) == (c >> (s.bit_length() - 1))
    size = SUB
    pows = [jnp.where(same(size), low, 0.0) for low in lows]
    invs = [eye - b for b in pows]
    order = 2
    while order < SUB:
        pows = [dot(p, p) for p in pows]
        invs = [i + dot(i, p) for i, p in zip(invs, pows)]
        order *= 2
        yield
    while size < n:
        offs = [jnp.where(same(2 * size), jnp.where(same(size), 0.0, low), 0.0) for low in lows]
        tmp = [dot(i, o) for i, o in zip(invs, offs)]
        yield
        invs = [i - dot(t, i) for i, t in zip(invs, tmp)]
        size *= 2
        yield
    return invs


def _deltanet_body(first, qkvz_ref, small_ref, conv_ref, alog_ref, dtb_ref, gn_ref, o_ref,
                   state_scr, ext_scr):
    hw = N_HEADS * HEAD_DIM
    nb = qkvz_ref.shape[0]
    chains = [(bi, h) for bi in range(nb) for h in range(N_HEADS)]

    @pl.when(first)
    def _():
        state_scr[...] = jnp.zeros_like(state_scr)
        ext_scr[:, 0:8, :] = jnp.zeros((nb, 8, 3 * hw), F32)

    for bi in range(nb):
        ext_scr[bi, 8:8 + CHUNK, :] = qkvz_ref[bi, :, 0:3 * hw]

    r = lax.broadcasted_iota(jnp.int32, (CHUNK, CHUNK), 0)
    c = lax.broadcasted_iota(jnp.int32, (CHUNK, CHUNK), 1)
    causal = r >= c
    strict = r > c
    tril = _tril_ones(CHUNK)

    beta_all, gam_all, gam_t = [], [], []
    for bi in range(nb):
        sm = small_ref[bi]
        beta_all.append(_sigmoid(sm))
        pre = sm + dtb_ref[...]
        softplus = jnp.maximum(pre, 0.0) + jnp.log(1.0 + jnp.exp(-jnp.abs(pre)))
        logdecay = -jnp.exp(alog_ref[...]) * softplus
        gam = _dot_sel(tril, logdecay)
        gam_all.append(gam)
        gam_t.append(jnp.concatenate([gam, jnp.zeros_like(gam)], axis=0).T)

    ext_rows = ext_scr.shape[1]
    sel_d = (lax.broadcasted_iota(jnp.int32, (CHUNK, CONV_TAPS * ext_rows), 1)
             - lax.broadcasted_iota(jnp.int32, (CHUNK, CONV_TAPS * ext_rows), 0) - (8 - (CONV_TAPS - 1)))
    sel = sel_d == 0
    for k in range(1, CONV_TAPS):
        sel = sel | (sel_d == k * (ext_rows + 1))
    sel = sel.astype(BF16)

    def conv(sec, bi):
        staged = ext_scr[bi, :, sec * hw:(sec + 1) * hw]
        taps = [staged * conv_ref[k:k + 1, sec * hw:(sec + 1) * hw] for k in range(CONV_TAPS)]
        out = _silu(_dot(sel, jnp.concatenate(taps, axis=0)))
        return [out[:, h * HEAD_DIM:(h + 1) * HEAD_DIM] for h in range(N_HEADS)]

    qs = [q for bi in range(nb) for q in conv(0, bi)]
    yield
    ks = [k for bi in range(nb) for k in conv(1, bi)]
    yield
    vs = [v for bi in range(nb) for v in conv(2, bi)]
    yield
    qs = [q * lax.rsqrt(jnp.sum(q * q, axis=-1, keepdims=True) + EPS) * (HEAD_DIM ** -0.5) for q in qs]
    ks = [k * lax.rsqrt(jnp.sum(k * k, axis=-1, keepdims=True) + EPS) for k in ks]

    yield
    betas = [beta_all[bi][:, h:h + 1] for bi, h in chains]
    gcols = [gam_all[bi][:, N_HEADS + h:N_HEADS + h + 1] for bi, h in chains]
    grows = [gam_t[bi][N_HEADS + h:N_HEADS + h + 1, 0:CHUNK] for bi, h in chains]
    egs = [jnp.exp(g) for g in gcols]
    decs = [jnp.exp(jnp.where(causal, gc - gr, -jnp.inf)) for gc, gr in zip(gcols, grows)]
    yield
    kbs = [k * b for k, b in zip(ks, betas)]
    lows = [jnp.where(strict, _dot_nt(kb, k) * d, 0.0) for kb, k, d in zip(kbs, ks, decs)]
    a_qks = [_dot_nt(q, k) * d for q, k, d in zip(qs, ks, decs)]
    yield
    invs = yield from _unit_lower_inverses(lows, _dot)
    yield
    sols = [_dot(i, jnp.concatenate([v * b, kb * eg], axis=-1))
            for i, v, b, kb, eg in zip(invs, vs, betas, kbs, egs)]

    yield
    states = [state_scr[n] for n in range(len(chains))]
    u_news = [sol[:, :HEAD_DIM] - _dot(sol[:, HEAD_DIM:], st) for sol, st in zip(sols, states)]
    yield
    outs = [_dot(q * eg, st) + _dot(a, un) for q, eg, st, a, un in zip(qs, egs, states, a_qks, u_news)]
    yield
    for n in range(len(chains)):
        k_dec = ks[n] * jnp.exp(gcols[n][CHUNK - 1:CHUNK, :] - gcols[n])
        state_scr[n] = egs[n][CHUNK - 1:CHUNK, :] * states[n] + _dot_tn(k_dec, u_news[n])
    yield
    for n, (bi, h) in enumerate(chains):
        lo = h * HEAD_DIM
        z = qkvz_ref[bi, :, 3 * hw + lo:3 * hw + lo + HEAD_DIM]
        o_ref[bi, :, lo:lo + HEAD_DIM] = _gated_head_norm(outs[n], z, gn_ref[...]).astype(o_ref.dtype)

    for bi in range(nb):
        ext_scr[bi, 0:8, :] = ext_scr[bi, CHUNK:CHUNK + 8, :]


def _hgrn2_body(first, qfiz_ref, lb_ref, gn_ref, o_ref, state_scr, q_scr, b_scr):
    hw = N_HEADS * HEAD_DIM
    nb = qfiz_ref.shape[0]
    chains = [(bi, h) for bi in range(nb) for h in range(N_HEADS)]
    heads = range(len(chains))
    col = lambda sec, h: slice(sec * hw + h * HEAD_DIM, sec * hw + (h + 1) * HEAD_DIM)
    n_sub = CHUNK // HSUB

    @pl.when(first)
    def _():
        state_scr[...] = jnp.zeros_like(state_scr)

    tril = _tril_ones(CHUNK)
    lane = lax.broadcasted_iota(jnp.int32, (HSUB, HEAD_DIM), 1)
    subl = lax.broadcasted_iota(jnp.int32, (HSUB, HEAD_DIM), 0)
    r = lax.broadcasted_iota(jnp.int32, (CHUNK, CHUNK), 0)
    c = lax.broadcasted_iota(jnp.int32, (CHUNK, CHUNK), 1)
    same_block = (r >> (HSUB.bit_length() - 1)) == (c >> (HSUB.bit_length() - 1))

    lbs = [lb_ref[:, col(0, h)] for _, h in chains]
    fs = [qfiz_ref[bi, :, col(1, h)] for bi, h in chains]
    vs = [qfiz_ref[bi, :, col(2, h)] for bi, h in chains]
    qs = [_silu(qfiz_ref[bi, :, col(0, h)]) * (HEAD_DIM ** -0.5) for bi, h in chains]
    yield
    log_sigs = [jnp.minimum(f, 0.0) - jnp.log(1.0 + jnp.exp(-jnp.abs(f))) for f in fs]
    las = [jnp.log(lb) for lb in lbs]
    l1s = [jnp.log(1.0 - lb) for lb in lbs]
    lcs = [l1 + ls for l1, ls in zip(l1s, log_sigs)]
    yield
    log_fs = [jnp.maximum(la, lc) + jnp.log(1.0 + jnp.exp(-jnp.abs(la - lc))) for la, lc in zip(las, lcs)]
    log_ks = [lc - f for lc, f in zip(lcs, fs)]
    kks = [jnp.exp(lk) for lk in log_ks]
    yield
    bcums = [_dot_sel(tril, lf) for lf in log_fs]
    bcs = [b - lk for b, lk in zip(bcums, log_ks)]
    yield
    states = [state_scr[h] for h in heads]
    inters = [_dot_nt(q * jnp.exp(b), st) for q, b, st in zip(qs, bcums, states)]
    for h in heads:
        q_scr[h] = qs[h]
        b_scr[h] = bcums[h]

    yield
    a_offs = []
    for h in heads:
        blocks = [jnp.zeros((HSUB, CHUNK), F32)]
        for i in range(1, n_sub):
            r0 = i * HSUB
            ref_row = bcums[h][r0:r0 + 1, :]
            q_in = qs[h][r0:r0 + HSUB, :] * jnp.exp(bcums[h][r0:r0 + HSUB, :] - ref_row)
            k_in = kks[h][0:r0, :] * jnp.exp(ref_row - bcums[h][0:r0, :])
            p = _dot_nt(q_in, k_in)
            blocks.append(jnp.concatenate([p, jnp.zeros((HSUB, CHUNK - r0), F32)], axis=1))
        a_offs.append(jnp.concatenate(blocks, axis=0))
        if h % 4 == 3:
            yield
    yield
    offs = [_dot(a, v) for a, v in zip(a_offs, vs)]

    yield
    diags = []
    for h in heads:
        a_t = jnp.zeros((HSUB, HEAD_DIM), F32)
        for i in range(n_sub):
            r0 = i * HSUB
            bc_i = bcs[h][r0:r0 + HSUB, :]
            for t in range(r0, r0 + HSUB):
                x = q_scr[h, t:t + 1, :] * jnp.exp(jnp.minimum(b_scr[h, t:t + 1, :] - bc_i, 0.0))
                a_t = jnp.where(lane == t, jnp.sum(x, axis=-1, keepdims=True), a_t)
        a_t = jnp.where(subl <= (lane & (HSUB - 1)), a_t, 0.0)
        a_full = jnp.where(same_block, jnp.concatenate([a_t[:, :CHUNK]] * n_sub, axis=0), 0.0)
        diags.append(_dot_tn(a_full, vs[h]))
        if h % 2:
            yield

    yield
    for h, (bi, hd) in enumerate(chains):
        b_last = bcums[h][CHUNK - 1:CHUNK, :]
        k_out = kks[h] * jnp.exp(b_last - bcums[h])
        state_scr[h] = states[h] * jnp.exp(b_last) + _dot_tn(vs[h], k_out)
        z = qfiz_ref[bi, :, col(3, hd)]
        o = inters[h] + offs[h] + diags[h]
        o_ref[bi, :, col(0, hd)] = _gated_head_norm(o, z, gn_ref[...]).astype(o_ref.dtype)


def _interleave(*stage_generators):
    pending = list(stage_generators)
    while pending:
        for gen in list(pending):
            try:
                next(gen)
            except StopIteration:
                pending.remove(gen)


def _rec_kernel(qkvz_ref, qfiz_ref, small_ref, conv_ref, alog_ref, dtb_ref, gna_ref, lb_ref, gnb_ref,
                oa_ref, ob_ref, dn_state, ext_scr, hg_state, q_scr, b_scr):
    first = pl.program_id(1) == 0
    _interleave(
        _deltanet_body(first, qkvz_ref, small_ref, conv_ref, alog_ref, dtb_ref, gna_ref, oa_ref, dn_state, ext_scr),
        _hgrn2_body(first, qfiz_ref, lb_ref, gnb_ref, ob_ref, hg_state, q_scr, b_scr))


def _rec(proj, small, conv_a, alog_row, dtb_row, gn_a, lb, gn_b, layer, nb=2):
    batch, seq, _ = proj.shape
    hw = N_HEADS * HEAD_DIM
    row = lambda width: pl.BlockSpec((None, 1, width), lambda b, n: (layer, 0, 0))
    return pl.pallas_call(
        _rec_kernel,
        name="rec",
        grid=(batch // nb, seq // CHUNK),
        in_specs=[
            pl.BlockSpec((nb, CHUNK, QKVZ_W), lambda b, n: (b, n, 0)),
            pl.BlockSpec((nb, CHUNK, QKVZ_W), lambda b, n: (b, n, 1)),
            pl.BlockSpec((nb, CHUNK, SMALL_W), lambda b, n: (b, n, 0)),
            pl.BlockSpec((None, CONV_TAPS, 3 * hw), lambda b, n: (layer, 0, 0)),
            row(SMALL_W), row(SMALL_W), row(HEAD_DIM), row(hw), row(HEAD_DIM),
        ],
        out_specs=[pl.BlockSpec((nb, CHUNK, hw), lambda b, n: (b, n, 0))] * 2,
        out_shape=[jax.ShapeDtypeStruct((batch, seq, hw), BF16)] * 2,
        scratch_shapes=[
            pltpu.VMEM((nb * N_HEADS, HEAD_DIM, HEAD_DIM), F32),
            pltpu.VMEM((nb, CHUNK + 8, 3 * hw), F32),
            pltpu.VMEM((nb * N_HEADS, HEAD_DIM, HEAD_DIM), F32),
            pltpu.VMEM((nb * N_HEADS, CHUNK, HEAD_DIM), F32),
            pltpu.VMEM((nb * N_HEADS, CHUNK, HEAD_DIM), F32),
        ],
        compiler_params=_params("parallel", "arbitrary"),
    )(proj, proj, small, conv_a, alog_row, dtb_row, gn_a, lb, gn_b)


def _memkv_kernel(m_ref, g_ref, w_ref, o_ref):
    h = _rms(m_ref[...], g_ref[...]).astype(BF16)
    o_ref[...] = jnp.dot(h, w_ref[...], preferred_element_type=F32).astype(o_ref.dtype)


def _memkv(mem2, g_mem, w_kv, layer, tn=512):
    m, d = mem2.shape
    n = w_kv.shape[-1]
    return pl.pallas_call(
        _memkv_kernel,
        name="memkv",
        grid=(n // tn,),
        in_specs=[
            pl.BlockSpec((m, d), lambda j: (0, 0)),
            pl.BlockSpec((None, 1, d), lambda j: (layer, 0, 0)),
            pl.BlockSpec((None, d, tn), lambda j: (layer, 0, j)),
        ],
        out_specs=pl.BlockSpec((m, tn), lambda j: (0, j)),
        out_shape=jax.ShapeDtypeStruct((m, n), BF16),
        compiler_params=_params("arbitrary"),
    )(mem2, g_mem, w_kv)


def _merge_kernel(oa_ref, ob_ref, q_ref, k_ref, v_ref, ga0, ga1, gb0, gb1, gc0, gc1,
                  wa_ref, wb_ref, wc_ref, o_ref):
    heads = []
    for h in range(MEM_HEADS):
        lo, hi = h * MEM_HEAD_DIM, (h + 1) * MEM_HEAD_DIM
        s = _dot_nt(q_ref[:, lo:hi], k_ref[:, lo:hi]) * (MEM_HEAD_DIM ** -0.5)
        p = jnp.exp(s - jnp.max(s, axis=-1, keepdims=True))
        p = p / jnp.sum(p, axis=-1, keepdims=True)
        heads.append(_dot(p, v_ref[:, lo:hi]))
    oc = jnp.concatenate(heads, axis=-1).astype(BF16)

    half = o_ref.shape[-1] // 2
    pa =jnp.dot(oa_ref[...], wa_ref[...], preferred_element_type=F32)
    pb = jnp.dot(ob_ref[...], wb_ref[...], preferred_element_type=F32)
    pc = jnp.dot(oc, wc_ref[...], preferred_element_type=F32)
    for part, (ra, rb, rc) in enumerate(((ga0, gb0, gc0), (ga1, gb1, gc1))):
        lo, hi = part * half, (part + 1) * half
        merged = (_sigmoid(ra[...]) * pa[:, lo:hi] + _sigmoid(rb[...]) * pb[:, lo:hi]
                  + _sigmoid(rc[...]) * pc[:, lo:hi])
        o_ref[:, lo:hi] = merged.astype(o_ref.dtype)


def _merge(oa, ob, proj, kv, w_a, w_b, w_c, layer, batch, n_mem, tm=256):
    t = oa.shape[0]
    d = w_a.shape[-1]
    cw = MEM_HEADS * MEM_HEAD_DIM
    half = d // 2
    tiles_per_batch = t // batch // tm
    q_blk = 2 * QKVZ_W // cw
    g_blk = (2 * QKVZ_W + cw) // half
    gate_specs = [pl.BlockSpec((tm, half), (lambda i, _j=j: (i, g_blk + _j))) for j in range(2 * N_BRANCH)]
    wspec = lambda: pl.BlockSpec((None, w_a.shape[1], d), lambda i: (layer, 0, 0))
    return pl.pallas_call(
        _merge_kernel,
        name="merge",
        grid=(t // tm,),
        in_specs=[
            pl.BlockSpec((tm, oa.shape[1]), lambda i: (i, 0)),
            pl.BlockSpec((tm, ob.shape[1]), lambda i: (i, 0)),
            pl.BlockSpec((tm, cw), lambda i: (i, q_blk)),
            pl.BlockSpec((n_mem, cw), lambda i: (i // tiles_per_batch, 0)),
            pl.BlockSpec((n_mem, cw), lambda i: (i // tiles_per_batch, 1)),
            *gate_specs,
            wspec(), wspec(), wspec(),
        ],
        out_specs=pl.BlockSpec((tm, d), lambda i: (i, 0)),
        out_shape=jax.ShapeDtypeStruct((t, d), BF16),
        compiler_params=_params("parallel"),
    )(oa, ob, proj, kv, kv, *([proj] * (2 * N_BRANCH)), w_a, w_b, w_c)


def _outproj_kernel(m_ref, w_ref, x_ref, g_ref, o_ref):
    y = jnp.dot(m_ref[...], w_ref[...], preferred_element_type=F32)
    o_ref[...] = x_ref[...] + _rms(y, g_ref[...])


def _outproj(merged, w_out, x2, g_post, layer, tm=512):
    t, d = x2.shape
    return pl.pallas_call(
        _outproj_kernel,
        name="outproj",
        grid=(t // tm,),
        in_specs=[
            pl.BlockSpec((tm, d), lambda i: (i, 0)),
            pl.BlockSpec((None, d, d), lambda i: (layer, 0, 0)),
            pl.BlockSpec((tm, d), lambda i: (i, 0)),
            pl.BlockSpec((None, 1, d), lambda i: (layer, 0, 0)),
        ],
        out_specs=pl.BlockSpec((tm, d), lambda i: (i, 0)),
        out_shape=jax.ShapeDtypeStruct((t, d), F32),
        compiler_params=_params("parallel"),
    )(merged, w_out, x2, g_post)


def _mlp_kernel(x_ref, gpre_ref, w1_ref, w2_ref, gpost_ref, gnext_ref, o_ref, hn_ref, h_scr):
    j = pl.program_id(1)

    @pl.when(j == 0)
    def _():
        h_scr[...] = _rms(x_ref[...], gpre_ref[...]).astype(BF16)
        o_ref[...] = jnp.zeros_like(o_ref)

    a = jnp.dot(h_scr[...], w1_ref[...], preferred_element_type=F32)
    a = jnp.maximum(a, 0.0)
    o_ref[...] += jnp.dot((a * a).astype(BF16), w2_ref[...], preferred_element_type=F32)

    @pl.when(j == pl.num_programs(1) - 1)
    def _():
        y = x_ref[...] + _rms(o_ref[...], gpost_ref[...])
        o_ref[...] = y
        hn_ref[...] = _rms(y, gnext_ref[...]).astype(hn_ref.dtype)


def _mlp(x2, g_pre, w1, w2, g_post, g_next, layer, next_layer, tm=512, tf=1024):
    t, d = x2.shape
    dff = w1.shape[-1]
    tm = min(tm, t)
    return pl.pallas_call(
        _mlp_kernel,
        name="mlp",
        grid=(t // tm, dff // tf),
        in_specs=[
            pl.BlockSpec((tm, d), lambda i, j: (i, 0)),
            pl.BlockSpec((None, 1, d), lambda i, j: (layer, 0, 0)),
            pl.BlockSpec((None, d, tf), lambda i, j: (layer, 0, j)),
            pl.BlockSpec((None, tf, d), lambda i, j: (layer, j, 0)),
            pl.BlockSpec((None, 1, d), lambda i, j: (layer, 0, 0)),
            pl.BlockSpec((None, 1, d), lambda i, j: (next_layer, 0, 0)),
        ],
        out_specs=[
            pl.BlockSpec((tm, d), lambda i, j: (i, 0)),
            pl.BlockSpec((tm, d), lambda i, j: (i, 0)),
        ],
        out_shape=[
            jax.ShapeDtypeStruct((t, d), F32),
            jax.ShapeDtypeStruct((t, d), BF16),
        ],
        scratch_shapes=[pltpu.VMEM((tm, d), BF16)],
        compiler_params=_params("parallel", "arbitrary"),
    )(x2, g_pre, w1, w2, g_post, g_next)


def kernel(x, mem, g_pre_mix, w_in, conv_a, a_log, dt_bias, gn_a, lb_raw, gn_b, g_mem, w_kv_mem, w_br_a, w_br_b, w_br_c, w_out, g_post_mix, g_pre_mlp, w_mlp_in, w_mlp_out, g_post_mlp):
    batch, seq, d = x.shape
    depth = w_in.shape[0]
    n_mem = mem.shape[1]
    hw = N_HEADS * HEAD_DIM
    x2 = x.reshape(batch * seq, d)
    mem2 = mem.reshape(batch * n_mem, d)

    small0 = QKVZ_W
    small1 = QKVZ_W + 2 * N_HEADS
    w_in_a = w_in[:, :, :small0].astype(BF16)
    w_in_rest = _repack_tail(w_in, small1)
    w_small = jnp.pad(w_in[:, :, small0:small1], ((0, 0), (0, 0), (0, SMALL_W - 2 * N_HEADS))).astype(BF16)
    pad_row = lambda p: jnp.pad(p, ((0, 0), (N_HEADS, SMALL_W - 2 * N_HEADS)))[:, None, :]
    alog_row = pad_row(a_log.astype(F32))
    dtb_row = pad_row(dt_bias.astype(F32))
    row = lambda p: p.astype(F32)[:, None, :]

    lb_all = jnp.cumsum(jax.nn.softmax(lb_raw.astype(F32), axis=0), axis=0)
    lb_all = (lb_all - lb_all[0])[:, None, :]

    w_kv = w_kv_mem.astype(BF16)
    w_a, w_b, w_c = w_br_a.astype(BF16), w_br_b.astype(BF16), w_br_c.astype(BF16)
    w_o = w_out.astype(BF16)
    w1, w2 = w_mlp_in.astype(BF16), w_mlp_out.astype(BF16)
    g_pre_mix, g_mem, g_post_mix, g_pre_mlp, g_post_mlp = map(row, (g_pre_mix, g_mem, g_post_mix, g_pre_mlp, g_post_mlp))
    gn_a, gn_b = row(gn_a), row(gn_b)
    conv_a = conv_a.astype(F32)

    h = _prenorm(x2, g_pre_mix, 0)
    for layer in range(depth):
        proj, small = _inproj(h, w_in_a, w_in_rest, w_small, layer)
        oa, ob = _rec(proj.reshape(batch, seq, -1), small.reshape(batch, seq, -1), conv_a, alog_row, dtb_row, gn_a,
                      lb_all, gn_b, layer)
        oa, ob = oa.reshape(batch * seq, hw), ob.reshape(batch * seq, hw)
        kv = _memkv(mem2, g_mem, w_kv, layer)
        merged = _merge(oa, ob, proj, kv, w_a, w_b, w_c, layer, batch, n_mem)
        x2 = _outproj(merged, w_o, x2, g_post_mix, layer)
        x2, h = _mlp(x2, g_pre_mlp, w1, w2, g_post_mlp, g_pre_mix, layer, (layer + 1) % depth)
    return x2.reshape(batch, seq, d)
```

```python
import functools

import jax
import jax.numpy as jnp
from jax import lax
from jax.experimental import pallas as pl
from jax.experimental.pallas import tpu as pltpu

F32 = jnp.float32
BF16 = jnp.bfloat16

EPS = 1e-6
CHUNK = 64
SUB = 4
HSUB = 8
HEAD_DIM = 128
N_HEADS = 8
CONV_TAPS = 4
MEM_HEADS = 4
MEM_HEAD_DIM = 256
N_BRANCH = 3
QKVZ_W = 4 * N_HEADS * HEAD_DIM
SMALL_W = 128
VMEM_LIMIT = 56 * 1024 * 1024


def _params(*sem):
    return pltpu.CompilerParams(dimension_semantics=sem, vmem_limit_bytes=VMEM_LIMIT)


def _sigmoid(x):
    return 1.0 / (1.0 + jnp.exp(-x))


def _silu(x):
    return x * _sigmoid(x)


def _rms(x, g):
    return x * lax.rsqrt(jnp.mean(x * x, axis=-1, keepdims=True) + EPS) * g


def _dot(a, b):
    return jnp.dot(a.astype(BF16), b.astype(BF16), preferred_element_type=F32)


def _dot_nt(a, b):
    return lax.dot_general(a.astype(BF16), b.astype(BF16), (((1,), (1,)), ((), ())),
                           preferred_element_type=F32)


def _dot_tn(a, b):
    return lax.dot_general(a.astype(BF16), b.astype(BF16), (((0,), (0,)), ((), ())),
                           preferred_element_type=F32)


def _split3(x):
    hi = x.astype(BF16)
    r1 = x - hi.astype(F32)
    mid = r1.astype(BF16)
    lo = (r1 - mid.astype(F32)).astype(BF16)
    return hi, mid, lo


def _dot_sel(sel, x):
    hi, mid, lo = _split3(x)
    d = lambda t: jnp.dot(sel, t, preferred_element_type=F32)
    return d(hi) + d(mid) + d(lo)


def _tril_ones(n):
    r = lax.broadcasted_iota(jnp.int32, (n, n), 0)
    c = lax.broadcasted_iota(jnp.int32, (n, n), 1)
    return (r >= c).astype(BF16)


def _prenorm_kernel(x_ref, g_ref, o_ref):
    o_ref[...] = _rms(x_ref[...], g_ref[...]).astype(o_ref.dtype)


def _prenorm(x2, g, layer, tm=1024):
    t, d = x2.shape
    tm = min(tm, t)
    return pl.pallas_call(
        _prenorm_kernel,
        name="prenorm",
        grid=(t // tm,),
        in_specs=[
            pl.BlockSpec((tm, d), lambda i: (i, 0)),
            pl.BlockSpec((None, 1, d), lambda i: (layer, 0, 0)),
        ],
        out_specs=pl.BlockSpec((tm, d), lambda i: (i, 0)),
        out_shape=jax.ShapeDtypeStruct((t, d), BF16),
        compiler_params=_params("parallel"),
    )(x2, g)


def _inproj_kernel(tiles_a, h_ref, wa_ref, wr_ref, ws_ref, o_ref, os_ref):
    j = pl.program_id(1)

    @pl.when(j == 0)
    def _():
        os_ref[...] = jnp.dot(h_ref[...], ws_ref[...], preferred_element_type=F32)

    @pl.when(j < tiles_a)
    def _():
        o_ref[...] = jnp.dot(h_ref[...], wa_ref[...], preferred_element_type=F32)

    @pl.when(j >= tiles_a)
    def _():
        o_ref[...] = jnp.dot(h_ref[...], wr_ref[...], preferred_element_type=F32)


def _inproj(h, w_a, w_rest, w_small, layer, tm=2048, tn=512):
    t, d = h.shape
    tiles_a = w_a.shape[-1] // tn
    tiles_rest = w_rest.shape[-1] // tn
    n = (tiles_a + tiles_rest) * tn
    tm = min(tm, t)
    return pl.pallas_call(
        functools.partial(_inproj_kernel, tiles_a),
        name="inproj",
        grid=(t // tm, tiles_a + tiles_rest),
        in_specs=[
            pl.BlockSpec((tm, d), lambda i, j: (i, 0)),
            pl.BlockSpec((None, d, tn), lambda i, j: (layer, 0, jnp.minimum(j, tiles_a - 1))),
            pl.BlockSpec((None, d, tn), lambda i, j: (layer, 0, jnp.maximum(j - tiles_a, 0))),
            pl.BlockSpec((None, d, SMALL_W), lambda i, j: (layer, 0, 0)),
        ],
        out_specs=[
            pl.BlockSpec((tm, tn), lambda i, j: (i, j)),
            pl.BlockSpec((tm, SMALL_W), lambda i, j: (i, 0)),
        ],
        out_shape=[
            jax.ShapeDtypeStruct((t, n), F32),
            jax.ShapeDtypeStruct((t, SMALL_W), F32),
        ],
        compiler_params=_params("parallel", "arbitrary"),
    )(h, w_a, w_rest, w_small)


def _gated_head_norm(o, z, g):
    return o * lax.rsqrt(jnp.mean(o * o, axis=-1, keepdims=True) + EPS) * g * _silu(z)


def _unit_lower_inverses(lows, dot):
    n = lows[0].shape[0]
    r = lax.broadcasted_iota(jnp.int32, (n, n), 0)
    c = lax.broadcasted_iota(jnp.int32, (n, n), 1)
    eye = (r == c).astype(F32)
    same = lambda s: (r >> (s.bit_length() - 1)) == (c >> (s.bit_length() - 1))
    size = SUB
    pows = [jnp.where(same(size), low, 0.0) for low in lows]
    invs = [eye - b for b in pows]
    order = 2
    while order < SUB:
        pows = [dot(p, p) for p in pows]
        invs = [i + dot(i, p) for i, p in zip(invs, pows)]
        order *= 2
        yield
    while size < n:
        offs = [jnp.where(same(2 * size), jnp.where(same(size), 0.0, low), 0.0) for low in lows]
        tmp = [dot(i, o) for i, o in zip(invs, offs)]
        yield
        invs = [i - dot(t, i) for i, t in zip(invs, tmp)]
        size *= 2
        yield
    return invs


def _deltanet_body(first, qkvz_ref, small_ref, conv_ref, alog_ref, dtb_ref, gn_ref, o_ref,
                   state_scr, ext_scr):
    hw = N_HEADS * HEAD_DIM
    nb = qkvz_ref.shape[0]
    chains = [(bi, h) for bi in range(nb) for h in range(N_HEADS)]

    @pl.when(first)
    def _():
        state_scr[...] = jnp.zeros_like(state_scr)
        ext_scr[:, 0:8, :] = jnp.zeros((nb, 8, 3 * hw), F32)

    for bi in range(nb):
        ext_scr[bi, 8:8 + CHUNK, :] = qkvz_ref[bi, :, 0:3 * hw]

    r = lax.broadcasted_iota(jnp.int32, (CHUNK, CHUNK), 0)
    c = lax.broadcasted_iota(jnp.int32, (CHUNK, CHUNK), 1)
    causal = r >= c
    strict = r > c
    tril = _tril_ones(CHUNK)

    beta_all, gam_all, gam_t = [], [], []
    for bi in range(nb):
        sm = small_ref[bi]
        beta_all.append(_sigmoid(sm))
        pre = sm + dtb_ref[...]
        softplus = jnp.maximum(pre, 0.0) + jnp.log(1.0 + jnp.exp(-jnp.abs(pre)))
        logdecay = -jnp.exp(alog_ref[...]) * softplus
        gam = _dot_sel(tril, logdecay)
        gam_all.append(gam)
        gam_t.append(jnp.concatenate([gam, jnp.zeros_like(gam)], axis=0).T)

    ext_rows = ext_scr.shape[1]
    sel_d = (lax.broadcasted_iota(jnp.int32, (CHUNK, CONV_TAPS * ext_rows), 1)
             - lax.broadcasted_iota(jnp.int32, (CHUNK, CONV_TAPS * ext_rows), 0) - (8 - (CONV_TAPS - 1)))
    sel = sel_d == 0
    for k in range(1, CONV_TAPS):
        sel = sel | (sel_d == k * (ext_rows + 1))
    sel = sel.astype(BF16)

    def conv(sec, bi):
        staged = ext_scr[bi, :, sec * hw:(sec + 1) * hw]
        taps = [staged * conv_ref[k:k + 1, sec * hw:(sec + 1) * hw] for k in range(CONV_TAPS)]
        out = _silu(_dot(sel, jnp.concatenate(taps, axis=0)))
        return [out[:, h * HEAD_DIM:(h + 1) * HEAD_DIM] for h in range(N_HEADS)]

    qs = [q for bi in range(nb) for q in conv(0, bi)]
    yield
    ks = [k for bi in range(nb) for k in conv(1, bi)]
    yield
    vs = [v for bi in range(nb) for v in conv(2, bi)]
    yield
    qs = [q * lax.rsqrt(jnp.sum(q * q, axis=-1, keepdims=True) + EPS) * (HEAD_DIM ** -0.5) for q in qs]
    ks = [k * lax.rsqrt(jnp.sum(k * k, axis=-1, keepdims=True) + EPS) for k in ks]

    yield
    betas = [beta_all[bi][:, h:h + 1] for bi, h in chains]
    gcols = [gam_all[bi][:, N_HEADS + h:N_HEADS + h + 1] for bi, h in chains]
    grows = [gam_t[bi][N_HEADS + h:N_HEADS + h + 1, 0:CHUNK] for bi, h in chains]
    egs = [jnp.exp(g) for g in gcols]
    decs = [jnp.exp(jnp.where(causal, gc - gr, -jnp.inf)) for gc, gr in zip(gcols, grows)]
    yield
    kbs = [k * b for k, b in zip(ks, betas)]
    lows = [jnp.where(strict, _dot_nt(kb, k) * d, 0.0) for kb, k, d in zip(kbs, ks, decs)]
    a_qks = [_dot_nt(q, k) * d for q, k, d in zip(qs, ks, decs)]
    yield
    invs = yield from _unit_lower_inverses(lows, _dot)
    yield
    sols = [_dot(i, jnp.concatenate([v * b, kb * eg], axis=-1))
            for i, v, b, kb, eg in zip(invs, vs, betas, kbs, egs)]

    yield
    states = [state_scr[n] for n in range(len(chains))]
    u_news = [sol[:, :HEAD_DIM] - _dot(sol[:, HEAD_DIM:], st) for sol, st in zip(sols, states)]
    yield
    outs = [_dot(q * eg, st) + _dot(a, un) for q, eg, st, a, un in zip(qs, egs, states, a_qks, u_news)]
    yield
    for n in range(len(chains)):
        k_dec = ks[n] * jnp.exp(gcols[n][CHUNK - 1:CHUNK, :] - gcols[n])
        state_scr[n] = egs[n][CHUNK - 1:CHUNK, :] * states[n] + _dot_tn(k_dec, u_news[n])
    yield
    for n, (bi, h) in enumerate(chains):
        lo = h * HEAD_DIM
        z = qkvz_ref[bi, :, 3 * hw + lo:3 * hw + lo + HEAD_DIM]
        o_ref[bi, :, lo:lo + HEAD_DIM] = _gated_head_norm(outs[n], z, gn_ref[...]).astype(o_ref.dtype)

    for bi in range(nb):
        ext_scr[bi, 0:8, :] = ext_scr[bi, CHUNK:CHUNK + 8, :]


def _hgrn2_body(first, qfiz_ref, lb_ref, gn_ref, o_ref, state_scr, q_scr, b_scr):
    hw = N_HEADS * HEAD_DIM
    nb = qfiz_ref.shape[0]
    chains = [(bi, h) for bi in range(nb) for h in range(N_HEADS)]
    heads = range(len(chains))
    col = lambda sec, h: slice(sec * hw + h * HEAD_DIM, sec * hw + (h + 1) * HEAD_DIM)
    n_sub = CHUNK // HSUB

    @pl.when(first)
    def _():
        state_scr[...] = jnp.zeros_like(state_scr)

    tril = _tril_ones(CHUNK)
    lane = lax.broadcasted_iota(jnp.int32, (HSUB, HEAD_DIM), 1)
    subl = lax.broadcasted_iota(jnp.int32, (HSUB, HEAD_DIM), 0)
    r = lax.broadcasted_iota(jnp.int32, (CHUNK, CHUNK), 0)
    c = lax.broadcasted_iota(jnp.int32, (CHUNK, CHUNK), 1)
    same_block = (r >> (HSUB.bit_length() - 1)) == (c >> (HSUB.bit_length() - 1))

    lbs = [lb_ref[:, col(0, h)] for _, h in chains]
    fs = [qfiz_ref[bi, :, col(1, h)] for bi, h in chains]
    vs = [qfiz_ref[bi, :, col(2, h)] for bi, h in chains]
    qs = [_silu(qfiz_ref[bi, :, col(0, h)]) * (HEAD_DIM ** -0.5) for bi, h in chains]
    yield
    log_sigs = [jnp.minimum(f, 0.0) - jnp.log(1.0 + jnp.exp(-jnp.abs(f))) for f in fs]
    las = [jnp.log(lb) for lb in lbs]
    l1s = [jnp.log(1.0 - lb) for lb in lbs]
    lcs = [l1 + ls for l1, ls in zip(l1s, log_sigs)]
    yield
    log_fs = [jnp.maximum(la, lc) + jnp.log(1.0 + jnp.exp(-jnp.abs(la - lc))) for la, lc in zip(las, lcs)]
    log_ks = [lc - f for lc, f in zip(lcs, fs)]
    kks = [jnp.exp(lk) for lk in log_ks]
    yield
    bcums = [_dot_sel(tril, lf) for lf in log_fs]
    bcs = [b - lk for b, lk in zip(bcums, log_ks)]
    yield
    states = [state_scr[h] for h in heads]
    inters = [_dot_nt(q * jnp.exp(b), st) for q, b, st in zip(qs, bcums, states)]
    for h in heads:
        q_scr[h] = qs[h]
        b_scr[h] = bcums[h]

    yield
    a_offs = []
    for h in heads:
        blocks = [jnp.zeros((HSUB, CHUNK), F32)]
        for i in range(1, n_sub):
            r0 = i * HSUB
            ref_row = bcums[h][r0:r0 + 1, :]
            q_in = qs[h][r0:r0 + HSUB, :] * jnp.exp(bcums[h][r0:r0 + HSUB, :] - ref_row)
            k_in = kks[h][0:r0, :] * jnp.exp(ref_row - bcums[h][0:r0, :])
            p = _dot_nt(q_in, k_in)
            blocks.append(jnp.concatenate([p, jnp.zeros((HSUB, CHUNK - r0), F32)], axis=1))
        a_offs.append(jnp.concatenate(blocks, axis=0))
        if h % 4 == 3:
            yield
    yield
    offs = [_dot(a, v) for a, v in zip(a_offs, vs)]

    yield
    diags = []
    for h in heads:
        a_t = jnp.zeros((HSUB, HEAD_DIM), F32)
        for i in range(n_sub):
            r0 = i * HSUB
            bc_i = bcs[h][r0:r0 + HSUB, :]
            for t in range(r0, r0 + HSUB):
                x = q_scr[h, t:t + 1, :] * jnp.exp(jnp.minimum(b_scr[h, t:t + 1, :] - bc_i, 0.0))
                a_t = jnp.where(lane == t, jnp.sum(x, axis=-1, keepdims=True), a_t)
        a_t = jnp.where(subl <= (lane & (HSUB - 1)), a_t, 0.0)
        a_full = jnp.where(same_block, jnp.concatenate([a_t[:, :CHUNK]] * n_sub, axis=0), 0.0)
        diags.append(_dot_tn(a_full, vs[h]))
        if h % 2:
            yield

    yield
    for h, (bi, hd) in enumerate(chains):
        b_last = bcums[h][CHUNK - 1:CHUNK, :]
        k_out = kks[h] * jnp.exp(b_last - bcums[h])
        state_scr[h] = states[h] * jnp.exp(b_last) + _dot_tn(vs[h], k_out)
        z = qfiz_ref[bi, :, col(3, hd)]
        o = inters[h] + offs[h] + diags[h]
        o_ref[bi, :, col(0, hd)] = _gated_head_norm(o, z, gn_ref[...]).astype(o_ref.dtype)


def _interleave(*stage_generators):
    pending = list(stage_generators)
    while pending:
        for gen in list(pending):
            try:
                next(gen)
            except StopIteration:
                pending.remove(gen)


def _rec_kernel(qkvz_ref, qfiz_ref, small_ref, conv_ref, alog_ref, dtb_ref, gna_ref, lb_ref, gnb_ref,
                oa_ref, ob_ref, dn_state, ext_scr, hg_state, q_scr, b_scr):
    first = pl.program_id(1) == 0
    _interleave(
        _deltanet_body(first, qkvz_ref, small_ref, conv_ref, alog_ref, dtb_ref, gna_ref, oa_ref, dn_state, ext_scr),
        _hgrn2_body(first, qfiz_ref, lb_ref, gnb_ref, ob_ref, hg_state, q_scr, b_scr))


def _rec(proj, small, conv_a, alog_row, dtb_row, gn_a, lb, gn_b, layer, nb=2):
    batch, seq, _ = proj.shape
    hw = N_HEADS * HEAD_DIM
    row = lambda width: pl.BlockSpec((None, 1, width), lambda b, n: (layer, 0, 0))
    return pl.pallas_call(
        _rec_kernel,
        name="rec",
        grid=(batch // nb, seq // CHUNK),
        in_specs=[
            pl.BlockSpec((nb, CHUNK, QKVZ_W), lambda b, n: (b, n, 0)),
            pl.BlockSpec((nb, CHUNK, QKVZ_W), lambda b, n: (b, n, 1)),
            pl.BlockSpec((nb, CHUNK, SMALL_W), lambda b, n: (b, n, 0)),
            pl.BlockSpec((None, CONV_TAPS, 3 * hw), lambda b, n: (layer, 0, 0)),
            row(SMALL_W), row(SMALL_W), row(HEAD_DIM), row(hw), row(HEAD_DIM),
        ],
        out_specs=[pl.BlockSpec((nb, CHUNK, hw), lambda b, n: (b, n, 0))] * 2,
        out_shape=[jax.ShapeDtypeStruct((batch, seq, hw), BF16)] * 2,
        scratch_shapes=[
            pltpu.VMEM((nb * N_HEADS, HEAD_DIM, HEAD_DIM), F32),
            pltpu.VMEM((nb, CHUNK + 8, 3 * hw), F32),
            pltpu.VMEM((nb * N_HEADS, HEAD_DIM, HEAD_DIM), F32),
            pltpu.VMEM((nb * N_HEADS, CHUNK, HEAD_DIM), F32),
            pltpu.VMEM((nb * N_HEADS, CHUNK, HEAD_DIM), F32),
        ],
        compiler_params=_params("parallel", "arbitrary"),
    )(proj, proj, small, conv_a, alog_row, dtb_row, gn_a, lb, gn_b)


def _memkv_kernel(m_ref, g_ref, w_ref, o_ref):
    h = _rms(m_ref[...], g_ref[...]).astype(BF16)
    o_ref[...] = jnp.dot(h, w_ref[...], preferred_element_type=F32).astype(o_ref.dtype)


def _memkv(mem2, g_mem, w_kv, layer, tn=512):
    m, d = mem2.shape
    n = w_kv.shape[-1]
    return pl.pallas_call(
        _memkv_kernel,
        name="memkv",
        grid=(n // tn,),
        in_specs=[
            pl.BlockSpec((m, d), lambda j: (0, 0)),
            pl.BlockSpec((None, 1, d), lambda j: (layer, 0, 0)),
            pl.BlockSpec((None, d, tn), lambda j: (layer, 0, j)),
        ],
        out_specs=pl.BlockSpec((m, tn), lambda j: (0, j)),
        out_shape=jax.ShapeDtypeStruct((m, n), BF16),
        compiler_params=_params("arbitrary"),
    )(mem2, g_mem, w_kv)


def _merge_kernel(oa_ref, ob_ref, q_ref, k_ref, v_ref, ga0, ga1, gb0, gb1, gc0, gc1,
                  wa_ref, wb_ref, wc_ref, o_ref):
    heads = []
    for h in range(MEM_HEADS):
        lo, hi = h * MEM_HEAD_DIM, (h + 1) * MEM_HEAD_DIM
        s = _dot_nt(q_ref[:, lo:hi], k_ref[:, lo:hi]) * (MEM_HEAD_DIM ** -0.5)
        p = jnp.exp(s - jnp.max(s, axis=-1, keepdims=True))
        p = p / jnp.sum(p, axis=-1, keepdims=True)
        heads.append(_dot(p, v_ref[:, lo:hi]))
    oc = jnp.concatenate(heads, axis=-1).astype(BF16)

    half = o_ref.shape[-1] // 2
    pa =jnp.dot(oa_ref[...], wa_ref[...], preferred_element_type=F32)
    pb = jnp.dot(ob_ref[...], wb_ref[...], preferred_element_type=F32)
    pc = jnp.dot(oc, wc_ref[...], preferred_element_type=F32)
    for part, (ra, rb, rc) in enumerate(((ga0, gb0, gc0), (ga1, gb1, gc1))):
        lo, hi = part * half, (part + 1) * half
        merged = (_sigmoid(ra[...]) * pa[:, lo:hi] + _sigmoid(rb[...]) * pb[:, lo:hi]
                  + _sigmoid(rc[...]) * pc[:, lo:hi])
        o_ref[:, lo:hi] = merged.astype(o_ref.dtype)


def _merge(oa, ob, proj, kv, w_a, w_b, w_c, layer, batch, n_mem, tm=256):
    t = oa.shape[0]
    d = w_a.shape[-1]
    cw = MEM_HEADS * MEM_HEAD_DIM
    half = d // 2
    tiles_per_batch = t // batch // tm
    q_blk = 2 * QKVZ_W // cw
    g_blk = (2 * QKVZ_W + cw) // half
    gate_specs = [pl.BlockSpec((tm, half), (lambda i, _j=j: (i, g_blk + _j))) for j in range(2 * N_BRANCH)]
    wspec = lambda: pl.BlockSpec((None, w_a.shape[1], d), lambda i: (layer, 0, 0))
    return pl.pallas_call(
        _merge_kernel,
        name="merge",
        grid=(t // tm,),
        in_specs=[
            pl.BlockSpec((tm, oa.shape[1]), lambda i: (i, 0)),
            pl.BlockSpec((tm, ob.shape[1]), lambda i: (i, 0)),
            pl.BlockSpec((tm, cw), lambda i: (i, q_blk)),
            pl.BlockSpec((n_mem, cw), lambda i: (i // tiles_per_batch, 0)),
            pl.BlockSpec((n_mem, cw), lambda i: (i // tiles_per_batch, 1)),
            *gate_specs,
            wspec(), wspec(), wspec(),
        ],
        out_specs=pl.BlockSpec((tm, d), lambda i: (i, 0)),
        out_shape=jax.ShapeDtypeStruct((t, d), BF16),
        compiler_params=_params("parallel"),
    )(oa, ob, proj, kv, kv, *([proj] * (2 * N_BRANCH)), w_a, w_b, w_c)


def _outproj_kernel(m_ref, w_ref, x_ref, g_ref, o_ref):
    y = jnp.dot(m_ref[...], w_ref[...], preferred_element_type=F32)
    o_ref[...] = x_ref[...] + _rms(y, g_ref[...])


def _outproj(merged, w_out, x2, g_post, layer, tm=512):
    t, d = x2.shape
    return pl.pallas_call(
        _outproj_kernel,
        name="outproj",
        grid=(t // tm,),
        in_specs=[
            pl.BlockSpec((tm, d), lambda i: (i, 0)),
            pl.BlockSpec((None, d, d), lambda i: (layer, 0, 0)),
            pl.BlockSpec((tm, d), lambda i: (i, 0)),
            pl.BlockSpec((None, 1, d), lambda i: (layer, 0, 0)),
        ],
        out_specs=pl.BlockSpec((tm, d), lambda i: (i, 0)),
        out_shape=jax.ShapeDtypeStruct((t, d), F32),
        compiler_params=_params("parallel"),
    )(merged, w_out, x2, g_post)


def _mlp_kernel(x_ref, gpre_ref, w1_ref, w2_ref, gpost_ref, gnext_ref, o_ref, hn_ref, h_scr):
    j = pl.program_id(1)

    @pl.when(j == 0)
    def _():
        h_scr[...] = _rms(x_ref[...], gpre_ref[...]).astype(BF16)
        o_ref[...] = jnp.zeros_like(o_ref)

    a = jnp.dot(h_scr[...], w1_ref[...], preferred_element_type=F32)
    a = jnp.maximum(a, 0.0)
    o_ref[...] += jnp.dot((a * a).astype(BF16), w2_ref[...], preferred_element_type=F32)

    @pl.when(j == pl.num_programs(1) - 1)
    def _():
        y = x_ref[...] + _rms(o_ref[...], gpost_ref[...])
        o_ref[...] = y
        hn_ref[...] = _rms(y, gnext_ref[...]).astype(hn_ref.dtype)


def _mlp(x2, g_pre, w1, w2, g_post, g_next, layer, next_layer, tm=512, tf=1024):
    t, d = x2.shape
    dff = w1.shape[-1]
    tm = min(tm, t)
    return pl.pallas_call(
        _mlp_kernel,
        name="mlp",
        grid=(t // tm, dff // tf),
        in_specs=[
            pl.BlockSpec((tm, d), lambda i, j: (i, 0)),
            pl.BlockSpec((None, 1, d), lambda i, j: (layer, 0, 0)),
            pl.BlockSpec((None, d, tf), lambda i, j: (layer, 0, j)),
            pl.BlockSpec((None, tf, d), lambda i, j: (layer, j, 0)),
            pl.BlockSpec((None, 1, d), lambda i, j: (layer, 0, 0)),
            pl.BlockSpec((None, 1, d), lambda i, j: (next_layer, 0, 0)),
        ],
        out_specs=[
            pl.BlockSpec((tm, d), lambda i, j: (i, 0)),
            pl.BlockSpec((tm, d), lambda i, j: (i, 0)),
        ],
        out_shape=[
            jax.ShapeDtypeStruct((t, d), F32),
            jax.ShapeDtypeStruct((t, d), BF16),
        ],
        scratch_shapes=[pltpu.VMEM((tm, d), BF16)],
        compiler_params=_params("parallel", "arbitrary"),
    )(x2, g_pre, w1, w2, g_post, g_next)


def kernel(x, mem, g_pre_mix, w_in, conv_a, a_log, dt_bias, gn_a, lb_raw, gn_b, g_mem, w_kv_mem, w_br_a, w_br_b, w_br_c, w_out, g_post_mix, g_pre_mlp, w_mlp_in, w_mlp_out, g_post_mlp):
    batch, seq, d = x.shape
    depth = w_in.shape[0]
    n_mem = mem.shape[1]
    hw = N_HEADS * HEAD_DIM
    x2 = x.reshape(batch * seq, d)
    mem2 = mem.reshape(batch * n_mem, d)

    small0 = QKVZ_W
    small1 = QKVZ_W + 2 * N_HEADS
    w_in_a = w_in[:, :, :small0].astype(BF16)
    w_in_rest = w_in[:, :, small1:].astype(BF16)
    w_small = jnp.pad(w_in[:, :, small0:small1], ((0, 0), (0, 0), (0, SMALL_W - 2 * N_HEADS))).astype(BF16)
    pad_row = lambda p: jnp.pad(p, ((0, 0), (N_HEADS, SMALL_W - 2 * N_HEADS)))[:, None, :]
    alog_row = pad_row(a_log.astype(F32))
    dtb_row = pad_row(dt_bias.astype(F32))
    row = lambda p: p.astype(F32)[:, None, :]

    lb_all = jnp.cumsum(jax.nn.softmax(lb_raw.astype(F32), axis=0), axis=0)
    lb_all = (lb_all - lb_all[0])[:, None, :]

    w_kv = w_kv_mem.astype(BF16)
    w_a, w_b, w_c = w_br_a.astype(BF16), w_br_b.astype(BF16), w_br_c.astype(BF16)
    w_o = w_out.astype(BF16)
    w1, w2 = w_mlp_in.astype(BF16), w_mlp_out.astype(BF16)
    g_pre_mix, g_mem, g_post_mix, g_pre_mlp, g_post_mlp = map(row, (g_pre_mix, g_mem, g_post_mix, g_pre_mlp, g_post_mlp))
    gn_a, gn_b = row(gn_a), row(gn_b)
    conv_a = conv_a.astype(F32)

    h = _prenorm(x2, g_pre_mix, 0)
    for layer in range(depth):
        proj, small = _inproj(h, w_in_a, w_in_rest, w_small, layer)
        oa, ob = _rec(proj.reshape(batch, seq, -1), small.reshape(batch, seq, -1), conv_a, alog_row, dtb_row, gn_a,
                      lb_all, gn_b, layer)
        oa, ob = oa.reshape(batch * seq, hw), ob.reshape(batch * seq, hw)
        kv = _memkv(mem2, g_mem, w_kv, layer)
        merged = _merge(oa, ob, proj, kv, w_a, w_b, w_c, layer, batch, n_mem)
        x2 = _outproj(merged, w_o, x2, g_post_mix, layer)
        x2, h = _mlp(x2, g_pre_mlp, w1, w2, g_post_mlp, g_pre_mix, layer, (layer + 1) % depth)
    return x2.reshape(batch, seq, d)
```

```python
import functools

import jax
import jax.numpy as jnp
from jax import lax
from jax.experimental import pallas as pl
from jax.experimental.pallas import tpu as pltpu

F32 = jnp.float32
BF16 = jnp.bfloat16

EPS = 1e-6
CHUNK = 64
SUB = 4
HSUB = 8
HEAD_DIM = 128
N_HEADS = 8
CONV_TAPS = 4
MEM_HEADS = 4
MEM_HEAD_DIM = 256
N_BRANCH = 3
QKVZ_W = 4 * N_HEADS * HEAD_DIM
SMALL_W = 128
VMEM_LIMIT = 56 * 1024 * 1024


def _params(*sem):
    return pltpu.CompilerParams(dimension_semantics=sem, vmem_limit_bytes=VMEM_LIMIT)


def _sigmoid(x):
    return 1.0 / (1.0 + jnp.exp(-x))


def _silu(x):
    return x * _sigmoid(x)


def _rms(x, g):
    return x * lax.rsqrt(jnp.mean(x * x, axis=-1, keepdims=True) + EPS) * g


def _dot(a, b):
    return jnp.dot(a.astype(BF16), b.astype(BF16), preferred_element_type=F32)


def _dot_nt(a, b):
    return lax.dot_general(a.astype(BF16), b.astype(BF16), (((1,), (1,)), ((), ())),
                           preferred_element_type=F32)


def _dot_tn(a, b):
    return lax.dot_general(a.astype(BF16), b.astype(BF16), (((0,), (0,)), ((), ())),
                           preferred_element_type=F32)


def _split3(x):
    hi = x.astype(BF16)
    r1 = x - hi.astype(F32)
    mid = r1.astype(BF16)
    lo = (r1 - mid.astype(F32)).astype(BF16)
    return hi, mid, lo


def _dot_sel(sel, x):
    hi, mid, lo = _split3(x)
    d = lambda t: jnp.dot(sel, t, preferred_element_type=F32)
    return d(hi) + d(mid) + d(lo)


def _tril_ones(n):
    r = lax.broadcasted_iota(jnp.int32, (n, n), 0)
    c = lax.broadcasted_iota(jnp.int32, (n, n), 1)
    return (r >= c).astype(BF16)


def _prenorm_kernel(x_ref, g_ref, o_ref):
    o_ref[...] = _rms(x_ref[...], g_ref[...]).astype(o_ref.dtype)


def _prenorm(x2, g, layer, tm=1024):
    t, d = x2.shape
    tm = min(tm, t)
    return pl.pallas_call(
        _prenorm_kernel,
        name="prenorm",
        grid=(t // tm,),
        in_specs=[
            pl.BlockSpec((tm, d), lambda i: (i, 0)),
            pl.BlockSpec((None, 1, d), lambda i: (layer, 0, 0)),
        ],
        out_specs=pl.BlockSpec((tm, d), lambda i: (i, 0)),
        out_shape=jax.ShapeDtypeStruct((t, d), BF16),
        compiler_params=_params("parallel"),
    )(x2, g)


def _inproj_kernel(tiles_a, h_ref, wa_ref, wr_ref, ws_ref, o_ref, os_ref):
    j = pl.program_id(1)

    @pl.when(j == 0)
    def _():
        os_ref[...] = jnp.dot(h_ref[...], ws_ref[...], preferred_element_type=F32)

    @pl.when(j < tiles_a)
    def _():
        o_ref[...] = jnp.dot(h_ref[...], wa_ref[...], preferred_element_type=F32)

    @pl.when(j >= tiles_a)
    def _():
        o_ref[...] = jnp.dot(h_ref[...], wr_ref[...], preferred_element_type=F32)


def _inproj(h, w_a, w_rest, w_small, layer, tm=2048, tn=512):
    t, d = h.shape
    tiles_a = w_a.shape[-1] // tn
    tiles_rest = w_rest.shape[-1] // tn
    n = (tiles_a + tiles_rest) * tn
    tm = min(tm, t)
    return pl.pallas_call(
        functools.partial(_inproj_kernel, tiles_a),
        name="inproj",
        grid=(t // tm, tiles_a + tiles_rest),
        in_specs=[
            pl.BlockSpec((tm, d), lambda i, j: (i, 0)),
            pl.BlockSpec((None, d, tn), lambda i, j: (layer, 0, jnp.minimum(j, tiles_a - 1))),
            pl.BlockSpec((None, d, tn), lambda i, j: (layer, 0, jnp.maximum(j - tiles_a, 0))),
            pl.BlockSpec((None, d, SMALL_W), lambda i, j: (layer, 0, 0)),
        ],
        out_specs=[
            pl.BlockSpec((tm, tn), lambda i, j: (i, j)),
            pl.BlockSpec((tm, SMALL_W), lambda i, j: (i, 0)),
        ],
        out_shape=[
            jax.ShapeDtypeStruct((t, n), F32),
            jax.ShapeDtypeStruct((t, SMALL_W), F32),
        ],
        compiler_params=_params("parallel", "arbitrary"),
    )(h, w_a, w_rest, w_small)


def _gated_head_norm(o, z, g):
    return o * lax.rsqrt(jnp.mean(o * o, axis=-1, keepdims=True) + EPS) * g * _silu(z)


def _unit_lower_inverses(lows, dot):
    n = lows[0].shape[0]
    r = lax.broadcasted_iota(jnp.int32, (n, n), 0)
    c = lax.broadcasted_iota(jnp.int32, (n, n), 1)
    eye = (r == c).astype(F32)
    same = lambda s: (r >> (s.bit_length() - 1)) == (c >> (s.bit_length() - 1))
    size = SUB
    pows = [jnp.where(same(size), low, 0.0) for low in lows]
    invs = [eye - b for b in pows]
    order = 2
    while order < SUB:
        pows = [dot(p, p) for p in pows]
        invs = [i + dot(i, p) for i, p in zip(invs, pows)]
        order *= 2
        yield
    while size < n:
        offs = [jnp.where(same(2 * size), jnp.where(same(size), 0.0, low), 0.0) for low in lows]
        tmp = [dot(i, o) for i, o in zip(invs, offs)]
        yield
        invs = [i - dot(t, i) for i, t in zip(invs, tmp)]
        size *= 2
        yield
    return invs


def _deltanet_body(first, qkvz_ref, small_ref, conv_ref, alog_ref, dtb_ref, gn_ref, o_ref,
                   state_scr, ext_scr):
    hw = N_HEADS * HEAD_DIM
    nb = qkvz_ref.shape[0]
    chains = [(bi, h) for bi in range(nb) for h in range(N_HEADS)]

    @pl.when(first)
    def _():
        state_scr[...] = jnp.zeros_like(state_scr)
        ext_scr[:, 0:8, :] = jnp.zeros((nb, 8, 3 * hw), F32)

    for bi in range(nb):
        ext_scr[bi, 8:8 + CHUNK, :] = qkvz_ref[bi, :, 0:3 * hw]

    r = lax.broadcasted_iota(jnp.int32, (CHUNK, CHUNK), 0)
    c = lax.broadcasted_iota(jnp.int32, (CHUNK, CHUNK), 1)
    causal = r >= c
    strict = r > c
    tril = _tril_ones(CHUNK)

    beta_all, gam_all, gam_t = [], [], []
    for bi in range(nb):
        sm = small_ref[bi]
        beta_all.append(_sigmoid(sm))
        pre = sm + dtb_ref[...]
        softplus = jnp.maximum(pre, 0.0) + jnp.log(1.0 + jnp.exp(-jnp.abs(pre)))
        logdecay = -jnp.exp(alog_ref[...]) * softplus
        gam = _dot_sel(tril, logdecay)
        gam_all.append(gam)
        gam_t.append(jnp.concatenate([gam, jnp.zeros_like(gam)], axis=0).T)

    ext_rows = ext_scr.shape[1]
    sel_d = (lax.broadcasted_iota(jnp.int32, (CHUNK, CONV_TAPS * ext_rows), 1)
             - lax.broadcasted_iota(jnp.int32, (CHUNK, CONV_TAPS * ext_rows), 0) - (8 - (CONV_TAPS - 1)))
    sel = sel_d == 0
    for k in range(1, CONV_TAPS):
        sel = sel | (sel_d == k * (ext_rows + 1))
    sel = sel.astype(BF16)

    def conv(sec, bi):
        staged = ext_scr[bi, :, sec * hw:(sec + 1) * hw]
        taps = [staged * conv_ref[k:k + 1, sec * hw:(sec + 1) * hw] for k in range(CONV_TAPS)]
        out = _silu(_dot(sel, jnp.concatenate(taps, axis=0)))
        return [out[:, h * HEAD_DIM:(h + 1) * HEAD_DIM] for h in range(N_HEADS)]

    qs = [q for bi in range(nb) for q in conv(0, bi)]
    yield
    ks = [k for bi in range(nb) for k in conv(1, bi)]
    yield
    vs = [v for bi in range(nb) for v in conv(2, bi)]
    yield
    qs = [q * lax.rsqrt(jnp.sum(q * q, axis=-1, keepdims=True) + EPS) * (HEAD_DIM ** -0.5) for q in qs]
    ks = [k * lax.rsqrt(jnp.sum(k * k, axis=-1, keepdims=True) + EPS) for k in ks]

    yield
    betas = [beta_all[bi][:, h:h + 1] for bi, h in chains]
    gcols = [gam_all[bi][:, N_HEADS + h:N_HEADS + h + 1] for bi, h in chains]
    grows = [gam_t[bi][N_HEADS + h:N_HEADS + h + 1, 0:CHUNK] for bi, h in chains]
    egs = [jnp.exp(g) for g in gcols]
    decs = [jnp.exp(jnp.where(causal, gc - gr, -jnp.inf)) for gc, gr in zip(gcols, grows)]
    yield
    kbs = [k * b for k, b in zip(ks, betas)]
    lows = [jnp.where(strict, _dot_nt(kb, k) * d, 0.0) for kb, k, d in zip(kbs, ks, decs)]
    a_qks = [_dot_nt(q, k) * d for q, k, d in zip(qs, ks, decs)]
    yield
    invs = yield from _unit_lower_inverses(lows, _dot)
    yield
    sols = [_dot(i, jnp.concatenate([v * b, kb * eg], axis=-1))
            for i, v, b, kb, eg in zip(invs, vs, betas, kbs, egs)]

    yield
    states = [state_scr[n] for n in range(len(chains))]
    u_news = [sol[:, :HEAD_DIM] - _dot(sol[:, HEAD_DIM:], st) for sol, st in zip(sols, states)]
    yield
    outs = [_dot(q * eg, st) + _dot(a, un) for q, eg, st, a, un in zip(qs, egs, states, a_qks, u_news)]
    yield
    for n in range(len(chains)):
        k_dec = ks[n] * jnp.exp(gcols[n][CHUNK - 1:CHUNK, :] - gcols[n])
        state_scr[n] = egs[n][CHUNK - 1:CHUNK, :] * states[n] + _dot_tn(k_dec, u_news[n])
    yield
    for n, (bi, h) in enumerate(chains):
        lo = h * HEAD_DIM
        z = qkvz_ref[bi, :, 3 * hw + lo:3 * hw + lo + HEAD_DIM]
        o_ref[bi, :, lo:lo + HEAD_DIM] = _gated_head_norm(outs[n], z, gn_ref[...]).astype(o_ref.dtype)

    for bi in range(nb):
        ext_scr[bi, 0:8, :] = ext_scr[bi, CHUNK:CHUNK + 8, :]


def _hgrn2_body(first, qfiz_ref, lb_ref, gn_ref, o_ref, state_scr, q_scr, b_scr):
    hw = N_HEADS * HEAD_DIM
    nb = qfiz_ref.shape[0]
    chains = [(bi, h) for bi in range(nb) for h in range(N_HEADS)]
    heads = range(len(chains))
    col = lambda sec, h: slice(sec * hw + h * HEAD_DIM, sec * hw + (h + 1) * HEAD_DIM)
    n_sub = CHUNK // HSUB

    @pl.when(first)
    def _():
        state_scr[...] = jnp.zeros_like(state_scr)

    tril = _tril_ones(CHUNK)
    lane = lax.broadcasted_iota(jnp.int32, (HSUB, HEAD_DIM), 1)
    subl = lax.broadcasted_iota(jnp.int32, (HSUB, HEAD_DIM), 0)
    r = lax.broadcasted_iota(jnp.int32, (CHUNK, CHUNK), 0)
    c = lax.broadcasted_iota(jnp.int32, (CHUNK, CHUNK), 1)
    same_block = (r >> (HSUB.bit_length() - 1)) == (c >> (HSUB.bit_length() - 1))

    lbs = [lb_ref[:, col(0, h)] for _, h in chains]
    fs = [qfiz_ref[bi, :, col(1, h)] for bi, h in chains]
    vs = [qfiz_ref[bi, :, col(2, h)] for bi, h in chains]
    qs = [_silu(qfiz_ref[bi, :, col(0, h)]) * (HEAD_DIM ** -0.5) for bi, h in chains]
    yield
    log_sigs = [jnp.minimum(f, 0.0) - jnp.log(1.0 + jnp.exp(-jnp.abs(f))) for f in fs]
    las = [jnp.log(lb) for lb in lbs]
    l1s = [jnp.log(1.0 - lb) for lb in lbs]
    lcs = [l1 + ls for l1, ls in zip(l1s, log_sigs)]
    yield
    log_fs = [jnp.maximum(la, lc) + jnp.log(1.0 + jnp.exp(-jnp.abs(la - lc))) for la, lc in zip(las, lcs)]
    log_ks = [lc - f for lc, f in zip(lcs, fs)]
    kks = [jnp.exp(lk) for lk in log_ks]
    yield
    bcums = [_dot_sel(tril, lf) for lf in log_fs]
    bcs = [b - lk for b, lk in zip(bcums, log_ks)]
    yield
    states = [state_scr[h] for h in heads]
    inters = [_dot_nt(q * jnp.exp(b), st) for q, b, st in zip(qs, bcums, states)]
    for h in heads:
        q_scr[h] = qs[h]
        b_scr[h] = bcums[h]

    yield
    a_offs = []
    for h in heads:
        blocks = [jnp.zeros((HSUB, CHUNK), F32)]
        for i in range(1, n_sub):
            r0 = i * HSUB
            ref_row = bcums[h][r0:r0 + 1, :]
            q_in = qs[h][r0:r0 + HSUB, :] * jnp.exp(bcums[h][r0:r0 + HSUB, :] - ref_row)
            k_in = kks[h][0:r0, :] * jnp.exp(ref_row - bcums[h][0:r0, :])
            p = _dot_nt(q_in, k_in)
            blocks.append(jnp.concatenate([p, jnp.zeros((HSUB, CHUNK - r0), F32)], axis=1))
        a_offs.append(jnp.concatenate(blocks, axis=0))
        if h % 4 == 3:
            yield
    yield
    offs = [_dot(a, v) for a, v in zip(a_offs, vs)]

    yield
    diags = []
    for h in heads:
        a_t = jnp.zeros((HSUB, HEAD_DIM), F32)
        for i in range(n_sub):
            r0 = i * HSUB
            bc_i = bcs[h][r0:r0 + HSUB, :]
            for t in range(r0, r0 + HSUB):
                x = q_scr[h, t:t + 1, :] * jnp.exp(jnp.minimum(b_scr[h, t:t + 1, :] - bc_i, 0.0))
                a_t = jnp.where(lane == t, jnp.sum(x, axis=-1, keepdims=True), a_t)
        a_t = jnp.where(subl <= (lane & (HSUB - 1)), a_t, 0.0)
        a_full = jnp.where(same_block, jnp.concatenate([a_t[:, :CHUNK]] * n_sub, axis=0), 0.0)
        diags.append(_dot_tn(a_full, vs[h]))
        if h % 2:
            yield

    yield
    for h, (bi, hd) in enumerate(chains):
        b_last = bcums[h][CHUNK - 1:CHUNK, :]
        k_out = kks[h] * jnp.exp(b_last - bcums[h])
        state_scr[h] = states[h] * jnp.exp(b_last) + _dot_tn(vs[h], k_out)
        z = qfiz_ref[bi, :, col(3, hd)]
        o = inters[h] + offs[h] + diags[h]
        o_ref[bi, :, col(0, hd)] = _gated_head_norm(o, z, gn_ref[...]).astype(o_ref.dtype)


def _interleave(*stage_generators):
    pending = list(stage_generators)
    while pending:
        for gen in list(pending):
            try:
                next(gen)
            except StopIteration:
                pending.remove(gen)


def _rec_kernel(qkvz_ref, qfiz_ref, small_ref, conv_ref, alog_ref, dtb_ref, gna_ref, lb_ref, gnb_ref,
                oa_ref, ob_ref, dn_state, ext_scr, hg_state, q_scr, b_scr):
    first = pl.program_id(1) == 0
    _interleave(
        _deltanet_body(first, qkvz_ref, small_ref, conv_ref, alog_ref, dtb_ref, gna_ref, oa_ref, dn_state, ext_scr),
        _hgrn2_body(first, qfiz_ref, lb_ref, gnb_ref, ob_ref, hg_state, q_scr, b_scr))


def _rec(proj, small, conv_a, alog_row, dtb_row, gn_a, lb, gn_b, layer, nb=2):
    batch, seq, _ = proj.shape
    hw = N_HEADS * HEAD_DIM
    row = lambda width: pl.BlockSpec((None, 1, width), lambda b, n: (layer, 0, 0))
    return pl.pallas_call(
        _rec_kernel,
        name="rec",
        grid=(batch // nb, seq // CHUNK),
        in_specs=[
            pl.BlockSpec((nb, CHUNK, QKVZ_W), lambda b, n: (b, n, 0)),
            pl.BlockSpec((nb, CHUNK, QKVZ_W), lambda b, n: (b, n, 1)),
            pl.BlockSpec((nb, CHUNK, SMALL_W), lambda b, n: (b, n, 0)),
            pl.BlockSpec((None, CONV_TAPS, 3 * hw), lambda b, n: (layer, 0, 0)),
            row(SMALL_W), row(SMALL_W), row(HEAD_DIM), row(hw), row(HEAD_DIM),
        ],
        out_specs=[pl.BlockSpec((nb, CHUNK, hw), lambda b, n: (b, n, 0))] * 2,
        out_shape=[jax.ShapeDtypeStruct((batch, seq, hw), BF16)] * 2,
        scratch_shapes=[
            pltpu.VMEM((nb * N_HEADS, HEAD_DIM, HEAD_DIM), F32),
            pltpu.VMEM((nb, CHUNK + 8, 3 * hw), F32),
            pltpu.VMEM((nb * N_HEADS, HEAD_DIM, HEAD_DIM), F32),
            pltpu.VMEM((nb * N_HEADS, CHUNK, HEAD_DIM), F32),
            pltpu.VMEM((nb * N_HEADS, CHUNK, HEAD_DIM), F32),
        ],
        compiler_params=_params("parallel", "arbitrary"),
    )(proj, proj, small, conv_a, alog_row, dtb_row, gn_a, lb, gn_b)


def _memkv_kernel(m_ref, g_ref, w_ref, o_ref):
    h = _rms(m_ref[...], g_ref[...]).astype(BF16)
    o_ref[...] = jnp.dot(h, w_ref[...], preferred_element_type=F32).astype(o_ref.dtype)


def _memkv(mem2, g_mem, w_kv, layer, tn=512):
    m, d = mem2.shape
    n = w_kv.shape[-1]
    return pl.pallas_call(
        _memkv_kernel,
        name="memkv",
        grid=(n // tn,),
        in_specs=[
            pl.BlockSpec((m, d), lambda j: (0, 0)),
            pl.BlockSpec((None, 1, d), lambda j: (layer, 0, 0)),
            pl.BlockSpec((None, d, tn), lambda j: (layer, 0, j)),
        ],
        out_specs=pl.BlockSpec((m, tn), lambda j: (0, j)),
        out_shape=jax.ShapeDtypeStruct((m, n), BF16),
        compiler_params=_params("arbitrary"),
    )(mem2, g_mem, w_kv)


def _merge_kernel(oa_ref, ob_ref, q_ref, k_ref, v_ref, ga0, ga1, gb0, gb1, gc0, gc1,
                  wa_ref, wb_ref, wc_ref, wo_ref, x_ref, gpost_ref, o_ref):
    heads = []
    for h in range(MEM_HEADS):
        lo, hi = h * MEM_HEAD_DIM, (h + 1) * MEM_HEAD_DIM
        s = _dot_nt(q_ref[:, lo:hi], k_ref[:, lo:hi]) * (MEM_HEAD_DIM ** -0.5)
        p = jnp.exp(s - jnp.max(s, axis=-1, keepdims=True))
        p = p / jnp.sum(p, axis=-1, keepdims=True)
        heads.append(_dot(p, v_ref[:, lo:hi]))
    oc = jnp.concatenate(heads, axis=-1).astype(BF16)

    half = o_ref.shape[-1] // 2
    pa =jnp.dot(oa_ref[...], wa_ref[...], preferred_element_type=F32)
    pb = jnp.dot(ob_ref[...], wb_ref[...], preferred_element_type=F32)
    pc = jnp.dot(oc, wc_ref[...], preferred_element_type=F32)
    parts = []
    for part, (ra, rb, rc) in enumerate(((ga0, gb0, gc0), (ga1, gb1, gc1))):
        lo, hi = part * half, (part + 1) * half
        merged = (_sigmoid(ra[...]) * pa[:, lo:hi] + _sigmoid(rb[...]) * pb[:, lo:hi]
                  + _sigmoid(rc[...]) * pc[:, lo:hi])
        parts.append(merged.astype(BF16))
    y = jnp.dot(jnp.concatenate(parts, axis=-1), wo_ref[...], preferred_element_type=F32)
    o_ref[...] = x_ref[...] + _rms(y, gpost_ref[...])


def _merge(oa, ob, proj, kv, w_a, w_b, w_c, w_out, x2, g_post, layer, batch, n_mem, tm=256):
    t = oa.shape[0]
    d = w_a.shape[-1]
    cw = MEM_HEADS * MEM_HEAD_DIM
    half = d // 2
    tiles_per_batch = t // batch // tm
    q_blk = 2 * QKVZ_W // cw
    g_blk = (2 * QKVZ_W + cw) // half
    gate_specs = [pl.BlockSpec((tm, half), (lambda i, _j=j: (i, g_blk + _j))) for j in range(2 * N_BRANCH)]
    wspec = lambda w: pl.BlockSpec((None, w.shape[1], d), lambda i: (layer, 0, 0), pipeline_mode=pl.Buffered(1))
    return pl.pallas_call(
        _merge_kernel,
        name="merge",
        grid=(t // tm,),
        in_specs=[
            pl.BlockSpec((tm, oa.shape[1]), lambda i: (i, 0)),
            pl.BlockSpec((tm, ob.shape[1]), lambda i: (i, 0)),
            pl.BlockSpec((tm, cw), lambda i: (i, q_blk)),
            pl.BlockSpec((n_mem, cw), lambda i: (i // tiles_per_batch, 0)),
            pl.BlockSpec((n_mem, cw), lambda i: (i // tiles_per_batch, 1)),
            *gate_specs,
            wspec(w_a), wspec(w_b), wspec(w_c), wspec(w_out),
            pl.BlockSpec((tm, d), lambda i: (i, 0)),
            pl.BlockSpec((None, 1, d), lambda i: (layer, 0, 0)),
        ],
        out_specs=pl.BlockSpec((tm, d), lambda i: (i, 0)),
        out_shape=jax.ShapeDtypeStruct((t, d), F32),
        compiler_params=_params("parallel"),
    )(oa, ob, proj, kv, kv, *([proj] * (2 * N_BRANCH)), w_a, w_b, w_c, w_out, x2, g_post)


def _mlp_kernel(x_ref, gpre_ref, w1_ref, w2_ref, gpost_ref, gnext_ref, o_ref, hn_ref, h_scr):
    j = pl.program_id(1)

    @pl.when(j == 0)
    def _():
        h_scr[...] = _rms(x_ref[...], gpre_ref[...]).astype(BF16)
        o_ref[...] = jnp.zeros_like(o_ref)

    a = jnp.dot(h_scr[...], w1_ref[...], preferred_element_type=F32)
    a = jnp.maximum(a, 0.0)
    o_ref[...] += jnp.dot((a * a).astype(BF16), w2_ref[...], preferred_element_type=F32)

    @pl.when(j == pl.num_programs(1) - 1)
    def _():
        y = x_ref[...] + _rms(o_ref[...], gpost_ref[...])
        o_ref[...] = y
        hn_ref[...] = _rms(y, gnext_ref[...]).astype(hn_ref.dtype)


def _mlp(x2, g_pre, w1, w2, g_post, g_next, layer, next_layer, tm=512, tf=1024):
    t, d = x2.shape
    dff = w1.shape[-1]
    tm = min(tm, t)
    return pl.pallas_call(
        _mlp_kernel,
        name="mlp",
        grid=(t // tm, dff // tf),
        in_specs=[
            pl.BlockSpec((tm, d), lambda i, j: (i, 0)),
            pl.BlockSpec((None, 1, d), lambda i, j: (layer, 0, 0)),
            pl.BlockSpec((None, d, tf), lambda i, j: (layer, 0, j)),
            pl.BlockSpec((None, tf, d), lambda i, j: (layer, j, 0)),
            pl.BlockSpec((None, 1, d), lambda i, j: (layer, 0, 0)),
            pl.BlockSpec((None, 1, d), lambda i, j: (next_layer, 0, 0)),
        ],
        out_specs=[
            pl.BlockSpec((tm, d), lambda i, j: (i, 0)),
            pl.BlockSpec((tm, d), lambda i, j: (i, 0)),
        ],
        out_shape=[
            jax.ShapeDtypeStruct((t, d), F32),
            jax.ShapeDtypeStruct((t, d), BF16),
        ],
        scratch_shapes=[pltpu.VMEM((tm, d), BF16)],
        compiler_params=_params("parallel", "arbitrary"),
    )(x2, g_pre, w1, w2, g_post, g_next)


def kernel(x, mem, g_pre_mix, w_in, conv_a, a_log, dt_bias, gn_a, lb_raw, gn_b, g_mem, w_kv_mem, w_br_a, w_br_b, w_br_c, w_out, g_post_mix, g_pre_mlp, w_mlp_in, w_mlp_out, g_post_mlp):
    batch, seq, d = x.shape
    depth = w_in.shape[0]
    n_mem = mem.shape[1]
    hw = N_HEADS * HEAD_DIM
    x2 = x.reshape(batch * seq, d)
    mem2 = mem.reshape(batch * n_mem, d)

    small0 = QKVZ_W
    small1 = QKVZ_W + 2 * N_HEADS
    w_in_a = w_in[:, :, :small0].astype(BF16)
    w_in_rest = w_in[:, :, small1:].astype(BF16)
    w_small = jnp.pad(w_in[:, :, small0:small1], ((0, 0), (0, 0), (0, SMALL_W - 2 * N_HEADS))).astype(BF16)
    pad_row = lambda p: jnp.pad(p, ((0, 0), (N_HEADS, SMALL_W - 2 * N_HEADS)))[:, None, :]
    alog_row = pad_row(a_log.astype(F32))
    dtb_row = pad_row(dt_bias.astype(F32))
    row = lambda p: p.astype(F32)[:, None, :]

    lb_all = jnp.cumsum(jax.nn.softmax(lb_raw.astype(F32), axis=0), axis=0)
    lb_all = (lb_all - lb_all[0])[:, None, :]

    w_kv = w_kv_mem.astype(BF16)
    w_a, w_b, w_c = w_br_a.astype(BF16), w_br_b.astype(BF16), w_br_c.astype(BF16)
    w_o = w_out.astype(BF16)
    w1, w2 = w_mlp_in.astype(BF16), w_mlp_out.astype(BF16)
    g_pre_mix, g_mem, g_post_mix, g_pre_mlp, g_post_mlp = map(row, (g_pre_mix, g_mem, g_post_mix, g_pre_mlp, g_post_mlp))
    gn_a, gn_b = row(gn_a), row(gn_b)
    conv_a = conv_a.astype(F32)

    h = _prenorm(x2, g_pre_mix, 0)
    for layer in range(depth):
        proj, small = _inproj(h, w_in_a, w_in_rest, w_small, layer)
        oa, ob = _rec(proj.reshape(batch, seq, -1), small.reshape(batch, seq, -1), conv_a, alog_row, dtb_row, gn_a,
                      lb_all, gn_b, layer)
        oa, ob = oa.reshape(batch * seq, hw), ob.reshape(batch * seq, hw)
        kv = _memkv(mem2, g_mem, w_kv, layer)
        x2 = _merge(oa, ob, proj, kv, w_a, w_b, w_c, w_o, x2, g_post_mix, layer, batch, n_mem)
        x2, h = _mlp(x2, g_pre_mlp, w1, w2, g_post_mlp, g_pre_mix, layer, (layer + 1) % depth)
    return x2.reshape(batch, seq, d)
```
